```python
import math
import jax
import jax.numpy as jnp
from jax import lax
import numpy as np

D_MODEL = 2048
BATCH = 8
SEQ = 2048
DEPTH = 4

GRID_W = 64
CTX_LEN = 256
N_MIXERS = 3
EPS = 1e-6
ROPE_BASE = 10000.0
Q_BLOCK = 128
NEG_INF = -1e30

DA_HEADS = 16
DA_DIM = D_MODEL // (2 * DA_HEADS)

DN_HK = 16
DN_HV = 32
DN_DK = D_MODEL // DN_HK
DN_DV = DN_DK
DN_CONV = 5
DN_CHUNK = 64
DN_QK = DN_HK * DN_DK
DN_VD = DN_HV * DN_DV
DN_QKV = 2 * DN_QK + DN_VD
DN_IN = DN_QKV + DN_VD + 4 * DN_HV

WA_HQ = 32
WA_HKV = 4
WA_GROUP = WA_HQ // WA_HKV
WA_DIM = D_MODEL // WA_HQ
WINDOW = 128
WA_QKV = (WA_HQ + 2 * WA_HKV) * WA_DIM

N_GROUPS = 4
EXPERTS_PER_GROUP = 8
N_EXPERTS = N_GROUPS * EXPERTS_PER_GROUP
TOP_K = 2
D_EXPERT = 3 * D_MODEL // 8
MOE_BLOCK = 128

kernel_name = 'hybrid_diffusion_block'


def layer_plan():
    counts = [0] * N_MIXERS
    plan = []
    for i in range(DEPTH):
        kind = i % N_MIXERS
        plan.append((kind, counts[kind]))
        counts[kind] += 1
    return plan, counts


def rms_norm(x, w):
    xf = x.astype(jnp.float32)
    y = xf * lax.rsqrt(jnp.mean(xf * xf, axis=-1, keepdims=True) + EPS)
    return (y * w.astype(jnp.float32)).astype(x.dtype)


def l2_normalize(x):
    return x * lax.rsqrt(jnp.sum(x * x, axis=-1, keepdims=True) + EPS)


def axial_rope_tables(n_tokens, dim):
    rows = n_tokens // GRID_W
    row, col = jnp.meshgrid(jnp.arange(rows), jnp.arange(GRID_W), indexing='ij')
    row = row.reshape(-1).astype(jnp.float32)
    col = col.reshape(-1).astype(jnp.float32)
    half = dim // 2
    inv_freq = 1.0 / (ROPE_BASE ** (jnp.arange(0, half, 2, dtype=jnp.float32) / half))
    def table(pos):
        ang = pos[:, None] * inv_freq[None, :]
        ang = jnp.concatenate([ang, ang], axis=-1)
        return jnp.cos(ang), jnp.sin(ang)
    cr, sr = table(row)
    cc, scol = table(col)
    return jnp.concatenate([cr, cc], -1), jnp.concatenate([sr, scol], -1)


def apply_axial_rope(x, cos, sin):
    dim = x.shape[-1]
    half = dim // 2
    shape = (1, cos.shape[0]) + (1,) * (x.ndim - 3) + (dim,)
    cos = cos.reshape(shape)
    sin = sin.reshape(shape)
    xf = x.astype(jnp.float32)
    def rot_half(u):
        u1, u2 = jnp.split(u, 2, axis=-1)
        return jnp.concatenate([-u2, u1], axis=-1)
    rotated = jnp.concatenate([rot_half(xf[..., :half]), rot_half(xf[..., half:])], -1)
    return (xf * cos + rotated * sin).astype(x.dtype)


def diff_attention(hx, hc, w_qkv, lam, subln_w, w_o, lambda_init, cos, sin, need_ctx):
    B, L, _ = hx.shape
    def project(h):
        n = h.shape[1]
        q, k, v = jnp.split(h @ w_qkv, 3, axis=-1)
        return (q.reshape(B, n, DA_HEADS, 2, DA_DIM),
                k.reshape(B, n, DA_HEADS, 2, DA_DIM),
                v.reshape(B, n, DA_HEADS, 2 * DA_DIM))
    qx, kx, vx = project(hx)
    qx = apply_axial_rope(qx, cos, sin)
    kx = apply_axial_rope(kx, cos, sin)
    qc, kc, vc = project(hc)
    lamf = lam.astype(jnp.float32)
    lam_full = (jnp.exp(jnp.sum(lamf[0] * lamf[1])) - jnp.exp(jnp.sum(lamf[2] * lamf[3]))
                + lambda_init)
    scale = DA_DIM ** -0.5
    def attend(q, k, v):
        s = jnp.einsum('bqhcd,bkhcd->bhcqk', q, k).astype(jnp.float32) * scale
        p = jax.nn.softmax(s, axis=-1)
        a = p[:, :, 0] - lam_full * p[:, :, 1]
        return jnp.einsum('bhqk,bkhe->bqhe', a.astype(v.dtype), v)
    k_all = jnp.concatenate([kc, kx], axis=1)
    v_all = jnp.concatenate([vc, vx], axis=1)
    nb = L // Q_BLOCK
    q_blocks = jnp.moveaxis(qx.reshape(B, nb, Q_BLOCK, DA_HEADS, 2, DA_DIM), 1, 0)
    ox = lax.map(lambda qb: attend(qb, k_all, v_all), q_blocks)
    ox = jnp.moveaxis(ox, 0, 1).reshape(B, L, DA_HEADS, 2 * DA_DIM)
    def head_out(o):
        n = o.shape[1]
        o = rms_norm(o, subln_w) * (1.0 - lambda_init)
        return o.reshape(B, n, D_MODEL) @ w_o
    out_c = head_out(attend(qc, kc, vc)) if need_ctx else None
    return head_out(ox), out_c


def depthwise_conv_centred(x, w):
    k, ch = w.shape
    return lax.conv_general_dilated(
        x, w.reshape(k, 1, ch).astype(x.dtype), window_strides=(1,),
        padding=[((k - 1) // 2, k // 2)], dimension_numbers=('NWC', 'WIO', 'NWC'),
        feature_group_count=ch)


def dn_stream(h, w_in, conv_w):
    B, n, _ = h.shape
    proj = h @ w_in
    qkv = jax.nn.silu(depthwise_conv_centred(proj[..., :DN_QKV], conv_w)).astype(jnp.float32)
    z = proj[..., DN_QKV:DN_QKV + DN_VD]
    ba = proj[..., DN_QKV + DN_VD:].astype(jnp.float32).reshape(B, n, 2, 2, DN_HV)
    q = l2_normalize(qkv[..., :DN_QK].reshape(B, n, DN_HK, DN_DK)) * (DN_DK ** -0.5)
    k = l2_normalize(qkv[..., DN_QK:2 * DN_QK].reshape(B, n, DN_HK, DN_DK))
    v = qkv[..., 2 * DN_QK:].reshape(B, n, DN_HV, DN_DV)
    rep = DN_HV // DN_HK
    return jnp.repeat(q, rep, axis=2), jnp.repeat(k, rep, axis=2), v, z, ba


def dn_gates(ba, a_log, dt_bias, d):
    beta = jax.nn.sigmoid(ba[:, :, d, 0])
    g = -jnp.exp(a_log[d].astype(jnp.float32)) * jax.nn.softplus(
        ba[:, :, d, 1] + dt_bias[d].astype(jnp.float32))
    return beta, g


def to_chunks(t):
    B, n, H = t.shape[:3]
    t = jnp.moveaxis(t, 2, 1)
    return t.reshape((B, H, n // DN_CHUNK, DN_CHUNK) + t.shape[3:])


def chunk_gated_delta_rule(q, k, v, beta, g, state):
    B, n, H, _ = q.shape
    q, k, v, beta, g = (to_chunks(t) for t in (q, k, v, beta, g))
    gc = jnp.cumsum(g, axis=-1)
    C = DN_CHUNK
    incl = jnp.tril(jnp.ones((C, C), bool))
    strict = jnp.tril(jnp.ones((C, C), bool), -1)
    decay = jnp.exp(jnp.where(incl, gc[..., :, None] - gc[..., None, :], -jnp.inf))
    kb = k * beta[..., None]
    lower = jnp.where(strict, jnp.einsum('bhnid,bhnjd->bhnij', kb, k) * decay, 0.0)
    a_mat = lower + jnp.eye(C, dtype=lower.dtype)
    rhs = jnp.concatenate([v * beta[..., None], kb * jnp.exp(gc)[..., None]], axis=-1)
    sol = lax.linalg.triangular_solve(a_mat, rhs, left_side=True, lower=True, unit_diagonal=True)
    u, w = sol[..., :DN_DV], sol[..., DN_DV:]
    intra = jnp.einsum('bhnid,bhnjd->bhnij', q, k) * decay
    q_dec = q * jnp.exp(gc)[..., None]
    k_dec = k * jnp.exp(gc[..., -1:] - gc)[..., None]
    g_tot = jnp.exp(gc[..., -1])
    xs = tuple(jnp.moveaxis(t, 2, 0) for t in (u, w, q_dec, k_dec, intra, g_tot))
    def step(S, inp):
        u_i, w_i, qd_i, kd_i, a_i, gt_i = inp
        v_new = u_i - jnp.einsum('bhck,bhkv->bhcv', w_i, S)
        o_i = (jnp.einsum('bhck,bhkv->bhcv', qd_i, S)
               + jnp.einsum('bhij,bhjv->bhiv', a_i, v_new))
        S = S * gt_i[..., None, None] + jnp.einsum('bhck,bhcv->bhkv', kd_i, v_new)
        return S, o_i
    state, o = lax.scan(step, state, xs)
    o = jnp.moveaxis(o, 0, 2).reshape(B, H, n, DN_DV)
    return jnp.moveaxis(o, 1, 2), state


def gated_deltanet(hx, hc, w_in, conv_w, a_log, dt_bias, norm_w, w_o, need_ctx):
    B, L, _ = hx.shape
    qx, kx, vx, zx, bax = dn_stream(hx, w_in, conv_w)
    qc, kc, vc, zc, bac = dn_stream(hc, w_in, conv_w)
    state0 = jnp.zeros((B, DN_HV, DN_DK, DN_DV), jnp.float32)
    outs_x = []
    outs_c = []
    for d in range(2):
        rev = (lambda t: jnp.flip(t, axis=1)) if d == 1 else (lambda t: t)
        beta_c, g_c = dn_gates(bac, a_log, dt_bias, d)
        beta_x, g_x = dn_gates(bax, a_log, dt_bias, d)
        o_c, s_c = chunk_gated_delta_rule(rev(qc), rev(kc), rev(vc), rev(beta_c), rev(g_c), state0)
        o_x, _ = chunk_gated_delta_rule(rev(qx), rev(kx), rev(vx), rev(beta_x), rev(g_x), s_c)
        outs_x.append(rev(o_x))
        if need_ctx:
            outs_c.append(rev(o_c))
    def head_out(o, z):
        n = o.shape[1]
        o = rms_norm(o, norm_w) * jax.nn.silu(z.astype(jnp.float32)).reshape(B, n, DN_HV, DN_DV)
        return o.reshape(B, n, DN_VD).astype(hx.dtype) @ w_o
    out_c = head_out(outs_c[0] + outs_c[1], zc) if need_ctx else None
    return head_out(outs_x[0] + outs_x[1], zx), out_c


def window_sink_gqa(hx, hc, w_qkv, sinks, w_o, cos, sin, need_ctx):
    B, L, _ = hx.shape
    Lc = hc.shape[1]
    def project(h):
        n = h.shape[1]
        qkv = h @ w_qkv
        q = qkv[..., :WA_HQ * WA_DIM].reshape(B, n, WA_HKV, WA_GROUP, WA_DIM)
        k = qkv[..., WA_HQ * WA_DIM:(WA_HQ + WA_HKV) * WA_DIM].reshape(B, n, WA_HKV, WA_DIM)
        v = qkv[..., (WA_HQ + WA_HKV) * WA_DIM:].reshape(B, n, WA_HKV, WA_DIM)
        return q, k, v
    qx, kx, vx = project(hx)
    qx = apply_axial_rope(qx, cos, sin)
    kx = apply_axial_rope(kx, cos, sin)
    qc, kc, vc = project(hc)
    scale = WA_DIM ** -0.5
    sink = sinks.astype(jnp.float32).reshape(1, WA_HKV, WA_GROUP, 1, 1)
    def softmax_with_sink(s):
        s_sink = jnp.broadcast_to(sink, s.shape[:-1] + (1,))
        return jax.nn.softmax(jnp.concatenate([s, s_sink], axis=-1), axis=-1)[..., :-1]
    def ctx_scores(q):
        return jnp.einsum('bqhgd,bkhd->bhgqk', q, kc).astype(jnp.float32) * scale
    band = Q_BLOCK + 2 * WINDOW
    k_pad = jnp.pad(kx, ((0, 0), (WINDOW, WINDOW), (0, 0), (0, 0)))
    v_pad = jnp.pad(vx, ((0, 0), (WINDOW, WINDOW), (0, 0), (0, 0)))
    def block(b):
        start = b * Q_BLOCK
        qb = lax.dynamic_slice_in_dim(qx, start, Q_BLOCK, axis=1)
        kb = lax.dynamic_slice_in_dim(k_pad, start, band, axis=1)
        vb = lax.dynamic_slice_in_dim(v_pad, start, band, axis=1)
        qpos = start + jnp.arange(Q_BLOCK)
        kpos = start - WINDOW + jnp.arange(band)
        valid = ((jnp.abs(qpos[:, None] - kpos[None, :]) <= WINDOW)
                 & (kpos >= 0)[None, :] & (kpos < L)[None, :])
        s_band = jnp.einsum('bqhgd,bkhd->bhgqk', qb, kb).astype(jnp.float32) * scale
        s_band = jnp.where(valid, s_band, NEG_INF)
        p = softmax_with_sink(jnp.concatenate([ctx_scores(qb), s_band], axis=-1)).astype(vb.dtype)
        return (jnp.einsum('bhgqk,bkhd->bqhgd', p[..., :Lc], vc)
                + jnp.einsum('bhgqk,bkhd->bqhgd', p[..., Lc:], vb))
    ox = lax.map(block, jnp.arange(L // Q_BLOCK))
    ox = jnp.moveaxis(ox, 0, 1).reshape(B, L, D_MODEL)
    out_c = None
    if need_ctx:
        p = softmax_with_sink(ctx_scores(qc)).astype(vc.dtype)
        oc = jnp.einsum('bhgqk,bkhd->bqhgd', p, vc).reshape(B, Lc, D_MODEL)
        out_c = oc @ w_o
    return ox @ w_o, out_c


def routed_experts(h, e_flat, g_flat, w13, w2):
    T, D = h.shape
    TK = e_flat.shape[0]
    tok = jnp.arange(TK) // TOP_K
    order = jnp.argsort(e_flat)
    e_sorted = e_flat[order]
    tok_sorted = tok[order]
    counts = jnp.zeros((N_EXPERTS,), jnp.int32).at[e_flat].add(1)
    padded = (counts + MOE_BLOCK - 1) // MOE_BLOCK * MOE_BLOCK
    start = jnp.cumsum(counts) - counts
    pstart = jnp.cumsum(padded) - padded
    dest = pstart[e_sorted] + jnp.arange(TK) - start[e_sorted]
    n_blocks = (TK + MOE_BLOCK - 1) // MOE_BLOCK + N_EXPERTS
    P = n_blocks * MOE_BLOCK
    buf_tok = jnp.zeros((P,), jnp.int32).at[dest].set(tok_sorted)
    xb = h[buf_tok].reshape(n_blocks, MOE_BLOCK, D)
    block_e = jnp.searchsorted(jnp.cumsum(padded), jnp.arange(n_blocks) * MOE_BLOCK, side='right')
    block_e = jnp.minimum(block_e, N_EXPERTS - 1)
    def one_block(args):
        xblk, e = args
        gate, up = jnp.split(xblk @ w13[e], 2, axis=-1)
        return (jax.nn.silu(gate) * up) @ w2[e]
    yb = lax.map(one_block, (xb, block_e)).reshape(P, D)
    y = yb[dest].astype(jnp.float32) * g_flat[order][:, None]
    return jnp.zeros((T, D), jnp.float32).at[tok_sorted].add(y).astype(h.dtype)


def hier_moe(h, wg, bg, we, be, w13, w2):
    T = h.shape[0]
    hf = h.astype(jnp.float32)
    rows = jnp.arange(T)
    pg = jax.nn.softmax(hf @ wg.astype(jnp.float32) + bg.astype(jnp.float32), axis=-1)
    g_sel = jnp.argmax(pg, axis=-1)
    p_grp = pg[rows, g_sel]
    fine = (hf @ we.astype(jnp.float32) + be.astype(jnp.float32)).reshape(
        T, N_GROUPS, EXPERTS_PER_GROUP)[rows, g_sel]
    top_p, top_i = lax.top_k(jax.nn.softmax(fine, axis=-1), TOP_K)
    gates = p_grp[:, None] * top_p / jnp.sum(top_p, axis=-1, keepdims=True)
    expert = g_sel[:, None] * EXPERTS_PER_GROUP + top_i
    return routed_experts(h, expert.reshape(-1), gates.reshape(-1), w13, w2)


def setup_inputs(seed: int = 0) -> dict:
    key = jax.random.key(seed)
    ks = iter(jax.random.split(key, 40))
    _, (n_a, n_b, n_c) = layer_plan()
    D = D_MODEL
    inv = D ** -0.5
    def normal(shape, scale):
        return jax.random.normal(next(ks), shape, jnp.float32) * scale
    def gain(shape):
        return 1.0 + normal(shape, 0.02)
    dt = jnp.exp(jax.random.uniform(next(ks), (n_b, 2, DN_HV), jnp.float32,
                                    minval=math.log(1e-3), maxval=math.log(1e-1)))
    a_init = jax.random.uniform(next(ks), (n_b, 2, DN_HV), jnp.float32, minval=1.0, maxval=16.0)
    return {
        'x': normal((BATCH, SEQ, D), 1.0),
        'c': normal((BATCH, D), 1.0),
        'ctx': normal((BATCH, CTX_LEN, D), 1.0),
        'c_ctx': normal((D,), 1.0),
        'ada_w': normal((DEPTH, D, 6 * D), 0.5 * inv),
        'ada_b': normal((DEPTH, 6 * D), 0.02),
        'norm_mix_w': gain((DEPTH, D)),
        'norm_ffn_w': gain((DEPTH, D)),
        'da_w_qkv': normal((n_a, D, 3 * D), inv),
        'da_lambda': normal((n_a, 4, DA_DIM), 0.1),
        'da_subln_w': gain((n_a, 2 * DA_DIM)),
        'da_w_o': normal((n_a, D, D), inv),
        'dn_w_in': normal((n_b, D, DN_IN), inv),
        'dn_conv_w': normal((n_b, DN_CONV, DN_QKV), DN_CONV ** -0.5),
        'dn_a_log': jnp.log(a_init),
        'dn_dt_bias': dt + jnp.log(-jnp.expm1(-dt)),
        'dn_norm_w': gain((n_b, DN_DV)),
        'dn_w_o': normal((n_b, DN_VD, D), DN_VD ** -0.5),
        'wa_w_qkv': normal((n_c, D, WA_QKV), inv),
        'wa_sinks': normal((n_c, WA_HQ), 0.5),
        'wa_w_o': normal((n_c, D, D), inv),
        'moe_wg': normal((DEPTH, D, N_GROUPS), inv),
        'moe_bg': normal((DEPTH, N_GROUPS), 0.01),
        'moe_we': normal((DEPTH, D, N_EXPERTS), inv),
        'moe_be': normal((DEPTH, N_EXPERTS), 0.01),
        'moe_w13': normal((DEPTH, N_EXPERTS, D, 2 * D_EXPERT), inv),
        'moe_w2': normal((DEPTH, N_EXPERTS, D_EXPERT, D), D_EXPERT ** -0.5),
        'final_norm_w': gain((D,)),
    }


def reference(x, c, ctx, c_ctx, ada_w, ada_b, norm_mix_w, norm_ffn_w,
              da_w_qkv, da_lambda, da_subln_w, da_w_o,
              dn_w_in, dn_conv_w, dn_a_log, dn_dt_bias, dn_norm_w, dn_w_o,
              wa_w_qkv, wa_sinks, wa_w_o,
              moe_wg, moe_bg, moe_we, moe_be, moe_w13, moe_w2, final_norm_w):
    plan, _ = layer_plan()
    B, L, D = x.shape
    Lc = ctx.shape[1]
    cos_a, sin_a = axial_rope_tables(L, DA_DIM)
    cos_w, sin_w = axial_rope_tables(L, WA_DIM)
    silu_c = jax.nn.silu(c.astype(jnp.float32))
    silu_cc = jax.nn.silu(c_ctx.astype(jnp.float32))
    xc = ctx
    for i, (kind, j) in enumerate(plan):
        last = i == DEPTH - 1
        need_ctx = not last
        w_ada = ada_w[i].astype(jnp.float32)
        b_ada = ada_b[i].astype(jnp.float32)
        mod_x = (silu_c @ w_ada + b_ada).astype(x.dtype)[:, None, :]
        mod_c = (silu_cc @ w_ada + b_ada).astype(x.dtype)
        sh1, sc1, g1, sh2, sc2, g2 = jnp.split(mod_x, 6, axis=-1)
        csh1, csc1, cg1, csh2, csc2, cg2 = jnp.split(mod_c, 6, axis=-1)
        hx = rms_norm(x, norm_mix_w[i]) * (1.0 + sc1) + sh1
        hc = rms_norm(xc, norm_mix_w[i]) * (1.0 + csc1) + csh1
        if kind == 0:
            lambda_init = 0.8 - 0.6 * math.exp(-0.3 * i)
            ox, oc = diff_attention(hx, hc, da_w_qkv[j], da_lambda[j], da_subln_w[j], da_w_o[j],
                                    lambda_init, cos_a, sin_a, need_ctx)
        elif kind == 1:
            ox, oc = gated_deltanet(hx, hc, dn_w_in[j], dn_conv_w[j], dn_a_log[j], dn_dt_bias[j],
                                    dn_norm_w[j], dn_w_o[j], need_ctx)
        else:
            ox, oc = window_sink_gqa(hx, hc, wa_w_qkv[j], wa_sinks[j], wa_w_o[j],
                                     cos_w, sin_w, need_ctx)
        x = x + g1 * ox
        if need_ctx:
            xc = xc + cg1 * oc
        hx = (rms_norm(x, norm_ffn_w[i]) * (1.0 + sc2) + sh2).reshape(B * L, D)
        moe = (moe_wg[i], moe_bg[i], moe_we[i], moe_be[i], moe_w13[i], moe_w2[i])
        if need_ctx:
            hc = (rms_norm(xc, norm_ffn_w[i]) * (1.0 + csc2) + csh2).reshape(B * Lc, D)
            y = hier_moe(jnp.concatenate([hx, hc], axis=0), *moe)
            x = x + g2 * y[:B * L].reshape(B, L, D)
            xc = xc + cg2 * y[B * L:].reshape(B, Lc, D)
        else:
            x = x + g2 * hier_moe(hx, *moe).reshape(B, L, D)
    return rms_norm(x, final_norm_w)
```

```python
import functools
import math

import jax
import jax.numpy as jnp
from jax import lax
from jax.experimental import pallas as pl
from jax.experimental.pallas import tpu as pltpu

F32 = jnp.float32
BF16 = jnp.bfloat16
HIGHEST = lax.Precision.HIGHEST

D_MODEL = 2048
DEPTH = 4
GRID_W = 64
CTX_LEN = 256
N_MIXERS = 3
EPS = 1e-6
ROPE_BASE = 10000.0
NEG_INF = -1e30
DA_HEADS = 16
DA_DIM = 64
DN_HK = 16
DN_HV = 32
DN_DK = 128
DN_CONV = 5
DN_CHUNK = 64
DN_QK = DN_HK * DN_DK
DN_VD = DN_HV * DN_DK
DN_QKV = 2 * DN_QK + DN_VD
WA_HQ = 32
WA_HKV = 4
WA_GROUP = 8
WA_DIM = 64
WINDOW = 128
N_GROUPS = 4
EXPERTS_PER_GROUP = 8
N_EXPERTS = 32
TOP_K = 2
D_EXPERT = 768

LANES = 128
TM = 256
MOE_ROWS = 256
DN_G = 8
VMEM_LIMIT = 56 * 1024 * 1024


def _cparams(n_axes):
    return pltpu.CompilerParams(dimension_semantics=("arbitrary",) * n_axes,
                                vmem_limit_bytes=VMEM_LIMIT)


def _mm(a, b):
    return jnp.dot(a.astype(BF16), b.astype(BF16), preferred_element_type=F32)


def _mm_nt(a, b):
    return lax.dot_general(a.astype(BF16), b.astype(BF16), (((1,), (1,)), ((), ())),
                           preferred_element_type=F32)


def _split_bf16(a):
    hi = a.astype(BF16)
    return hi, (a - hi.astype(F32)).astype(BF16)


def _mm_split(a, b):
    m = a.shape[0]
    a_hi, a_lo = _split_bf16(a)
    b_hi, b_lo = _split_bf16(b)
    lhs = jnp.concatenate([a_hi, a_lo], axis=0)
    o_hi = jnp.dot(lhs, b_hi, preferred_element_type=F32)
    o_lo = jnp.dot(lhs, b_lo, preferred_element_type=F32)
    return (o_hi[:m] + o_hi[m:]) + (o_lo[:m] + o_lo[m:])


def _mod_spec(nb, tpb, chunk, width):
    def imap(r, *_):
        return (jnp.where(r % tpb == 0, nb, r // tpb), 0, chunk)
    return pl.BlockSpec((1, 1, width), imap)


def _ada_kernel(c_ref, w_ref, b_ref, o_ref):
    c = c_ref[...]
    s = c * (1.0 / (1.0 + jnp.exp(-c)))
    o_ref[0] = jnp.dot(s, w_ref[0], preferred_element_type=F32, precision=HIGHEST) + b_ref[0]


def ada_tables(cc, ada_w, ada_b):
    depth, d, n = ada_w.shape
    tn = 768
    return pl.pallas_call(
        _ada_kernel,
        grid=(depth, n // tn),
        in_specs=[pl.BlockSpec((16, d), lambda l, j: (0, 0)),
                  pl.BlockSpec((1, d, tn), lambda l, j: (l, 0, j)),
                  pl.BlockSpec((1, 1, tn), lambda l, j: (l, 0, j))],
        out_specs=pl.BlockSpec((1, 16, tn), lambda l, j: (l, 0, j)),
        out_shape=jax.ShapeDtypeStruct((depth, 16, n), F32),
        compiler_params=_cparams(2),
        name="ada_tables",
    )(cc, ada_w, ada_b.reshape(depth, 1, n))


def _norm_mod(x, nw, sc, sh):
    y = x * lax.rsqrt(jnp.mean(x * x, axis=-1, keepdims=True) + EPS) * nw
    return y * (1.0 + sc) + sh


def _proj_kernel(*refs, rope_tiles):
    if rope_tiles:
        x_ref, nw_ref, sc_ref, sh_ref, w_ref, cos_ref, sa_ref, sb_ref, o_ref, h_scr = refs
    else:
        x_ref, nw_ref, sc_ref, sh_ref, w_ref, o_ref, h_scr = refs
    j = pl.program_id(1)

    @pl.when(j == 0)
    def _():
        h_scr[...] = _norm_mod(x_ref[...], nw_ref[...], sc_ref[0], sh_ref[0]).astype(h_scr.dtype)

    precision = HIGHEST if w_ref.dtype == F32 else None
    acc = jnp.dot(h_scr[...], w_ref[...], preferred_element_type=F32, precision=precision)
    if rope_tiles:
        tn = acc.shape[1]

        @pl.when(j < rope_tiles)
        def _():
            rot = (acc * cos_ref[...] + pltpu.roll(acc, tn - 16, 1) * sa_ref[...]
                   + pltpu.roll(acc, 16, 1) * sb_ref[...])
            o_ref[...] = rot.astype(o_ref.dtype)

        @pl.when(j >= rope_tiles)
        def _():
            o_ref[...] = acc.astype(o_ref.dtype)
    else:
        o_ref[...] = acc.astype(o_ref.dtype)


def proj(s2d, nw, mod, w, *, nb, sc_chunk, sh_chunk, tn, out_dtype, rope=None, rope_tiles=0):
    t, d = s2d.shape
    n = w.shape[1]
    tpb = t // nb // TM
    in_specs = [pl.BlockSpec((TM, d), lambda r, j: (r, 0)),
                pl.BlockSpec((1, d), lambda r, j: (0, 0)),
                _mod_spec(nb, tpb, sc_chunk, d),
                _mod_spec(nb, tpb, sh_chunk, d),
                pl.BlockSpec((d, tn), lambda r, j: (0, j))]
    args = [s2d, nw.reshape(1, d), mod, mod, w]
    if rope_tiles:
        tab = pl.BlockSpec((TM, tn), lambda r, j: (r % tpb, 0))
        in_specs += [tab, tab, tab]
        args += list(rope)
    return pl.pallas_call(
        functools.partial(_proj_kernel, rope_tiles=rope_tiles),
        grid=(t // TM, n // tn),
        in_specs=in_specs,
        out_specs=pl.BlockSpec((TM, tn), lambda r, j: (r, j)),
        out_shape=jax.ShapeDtypeStruct((t, n), out_dtype),
        scratch_shapes=[pltpu.VMEM((TM, d), w.dtype)],
        compiler_params=_cparams(2),
        name="proj",
    )(*args)


def _out_kernel(a_ref, w_ref, g_ref, s_ref, o_ref):
    acc = jnp.dot(a_ref[...], w_ref[...], preferred_element_type=F32)
    o_ref[...] = s_ref[...] + g_ref[0] * acc


def out_proj(a, w, mod, s2d, *, nb, gate_chunk, tn=1024):
    t, k = a.shape
    d = w.shape[1]
    tpb = t // nb // TM
    gspec = pl.BlockSpec((1, 1, tn), lambda r, j: (jnp.where(r % tpb == 0, nb, r // tpb), 0,
                                                   gate_chunk * (d // tn) + j))
    return pl.pallas_call(
        _out_kernel,
        grid=(t // TM, d // tn),
        in_specs=[pl.BlockSpec((TM, k), lambda r, j: (r, 0)),
                  pl.BlockSpec((k, tn), lambda r, j: (0, j)),
                  gspec,
                  pl.BlockSpec((TM, tn), lambda r, j: (r, j))],
        out_specs=pl.BlockSpec((TM, tn), lambda r, j: (r, j)),
        out_shape=jax.ShapeDtypeStruct((t, d), F32),
        compiler_params=_cparams(2),
        name="out_proj",
    )(a, w, mod, s2d)


def _softmax_rows(s):
    e = jnp.exp(s - jnp.max(s, axis=-1, keepdims=True))
    return e / jnp.sum(e, axis=-1, keepdims=True)


def _da_kernel(lam_ref, q_ref, k_ref, v_ref, sw_ref, o_ref, *, lambda_init, n_ctx):
    i = pl.program_id(2)
    lam = lam_ref[...]
    l01 = jnp.sum(lam[0:1] * lam[1:2], axis=-1, keepdims=True)
    l23 = jnp.sum(lam[2:3] * lam[3:4], axis=-1, keepdims=True)
    lam_full = jnp.exp(l01) - jnp.exp(l23) + lambda_init
    scale = DA_DIM ** -0.5
    lane = lax.broadcasted_iota(jnp.int32, (1, LANES), 1)

    def attend(nk):
        q = q_ref[...]
        k = k_ref[0:nk, :]
        v = v_ref[0:nk, :]
        zero = jnp.zeros_like(k)
        s1 = _mm_nt(q, jnp.where(lane < DA_DIM, k, zero)) * scale
        s2 = _mm_nt(q, jnp.where(lane >= DA_DIM, k, zero)) * scale
        a = _softmax_rows(s1) - lam_full * _softmax_rows(s2)
        o = _mm(a, v)
        o = o * lax.rsqrt(jnp.mean(o * o, axis=-1, keepdims=True) + EPS) * sw_ref[...]
        o_ref[...] = (o * (1.0 - lambda_init)).astype(o_ref.dtype)

    @pl.when(i == 0)
    def _():
        attend(n_ctx)

    @pl.when(i > 0)
    def _():
        attend(k_ref.shape[0])


def diff_attention(qkv, lam, subln_w, *, nb, lambda_init):
    t = qkv.shape[0]
    lt = t // nb
    tq = TM
    nq = lt // tq
    return pl.pallas_call(
        functools.partial(_da_kernel, lambda_init=lambda_init, n_ctx=CTX_LEN),
        grid=(nb, DA_HEADS, nq),
        in_specs=[pl.BlockSpec((4, DA_DIM), lambda b, h, i: (0, 0)),
                  pl.BlockSpec((tq, LANES), lambda b, h, i: (b * nq + i, h)),
                  pl.BlockSpec((lt, LANES), lambda b, h, i: (b, DA_HEADS + h)),
                  pl.BlockSpec((lt, LANES), lambda b, h, i: (b, 2 * DA_HEADS + h)),
                  pl.BlockSpec((1, LANES), lambda b, h, i: (0, 0))],
        out_specs=pl.BlockSpec((tq, LANES), lambda b, h, i: (b * nq + i, h)),
        out_shape=jax.ShapeDtypeStruct((t, D_MODEL), BF16),
        compiler_params=_cparams(3),
        name="diff_attention",
    )(lam, qkv, qkv, qkv, subln_w.reshape(1, LANES))


def _wa_kernel(sink_ref, q_ref, k_ref, v_ref, o_ref, klo, khi, vlo, vhi, *, n_ctx, seq):
    hkv = pl.program_id(1)
    i = pl.program_id(2)
    tq = q_ref.shape[0]
    band = tq + 2 * WINDOW
    scale = WA_DIM ** -0.5
    lane = lax.broadcasted_iota(jnp.int32, (1, LANES), 1)
    n_ctx_tiles = n_ctx // tq

    @pl.when(i == 0)
    def _():
        odd = (hkv % 2) == 1
        for src, lo, hi in ((k_ref, klo, khi), (v_ref, vlo, vhi)):
            x = src[...].astype(F32)
            x = jnp.where(odd, pltpu.roll(x, WA_DIM, 1), x)
            x = jnp.where(lane < WA_DIM, x, 0.0)
            lo[...] = x.astype(BF16)
            hi[...] = pltpu.roll(x, WA_DIM, 1).astype(BF16)

    def head_pairs(with_band):
        if with_band:
            lat0 = (i - n_ctx_tiles) * tq
            start = jnp.clip(lat0 - WINDOW, 0, seq - band)
            row0 = pl.multiple_of(n_ctx + start, LANES)
            qpos = lat0 + lax.broadcasted_iota(jnp.int32, (tq, band), 0)
            kpos = start + lax.broadcasted_iota(jnp.int32, (tq, band), 1)
            valid = jnp.abs(qpos - kpos) <= WINDOW
        for p in range(WA_GROUP // 2):
            q2 = q_ref[:, p * LANES:(p + 1) * LANES]
            acc = jnp.zeros((tq, LANES), F32)
            for par, (kx, vx) in enumerate(((klo, vlo), (khi, vhi))):
                sink = sink_ref[hkv, 2 * p + par]
                s_c = _mm_nt(q2, kx[0:n_ctx, :]) * scale
                m = jnp.maximum(jnp.max(s_c, axis=-1, keepdims=True), sink)
                if with_band:
                    s_b = _mm_nt(q2, kx[pl.ds(row0, band), :]) * scale
                    s_b = jnp.where(valid, s_b, NEG_INF)
                    m = jnp.maximum(m, jnp.max(s_b, axis=-1, keepdims=True))
                    e_b = jnp.exp(s_b - m)
                e_c = jnp.exp(s_c - m)
                den = jnp.sum(e_c, axis=-1, keepdims=True) + jnp.exp(sink - m)
                if with_band:
                    den = den + jnp.sum(e_b, axis=-1, keepdims=True)
                acc = acc + _mm(e_c / den, vx[0:n_ctx, :])
                if with_band:
                    acc = acc + _mm(e_b / den, vx[pl.ds(row0, band), :])
            o_ref[:, p * LANES:(p + 1) * LANES] = acc.astype(o_ref.dtype)

    @pl.when(i < n_ctx_tiles)
    def _():
        head_pairs(False)

    @pl.when(i >= n_ctx_tiles)
    def _():
        head_pairs(True)


def window_attention(qkv, sinks, *, nb):
    t = qkv.shape[0]
    lt = t // nb
    tq = 128
    nq = lt // tq
    qw = WA_GROUP * WA_DIM
    kblk = WA_HQ * WA_DIM // LANES
    vblk = kblk + WA_HKV * WA_DIM // LANES
    return pl.pallas_call(
        functools.partial(_wa_kernel, n_ctx=CTX_LEN, seq=lt - CTX_LEN),
        grid=(nb, WA_HKV, nq),
        in_specs=[pl.BlockSpec(memory_space=pltpu.SMEM),
                  pl.BlockSpec((tq, qw), lambda b, h, i: (b * nq + i, h)),
                  pl.BlockSpec((lt, LANES), lambda b, h, i: (b, kblk + h // 2)),
                  pl.BlockSpec((lt, LANES), lambda b, h, i: (b, vblk + h // 2))],
        out_specs=pl.BlockSpec((tq, qw), lambda b, h, i: (b * nq + i, h)),
        out_shape=jax.ShapeDtypeStruct((t, D_MODEL), BF16),
        scratch_shapes=[pltpu.VMEM((lt, LANES), BF16)] * 4,
        compiler_params=_cparams(3),
        name="window_attention",
    )(sinks.reshape(WA_HKV, WA_GROUP), qkv, qkv, qkv)


def _dn_conv_kernel(x_ref, w_ref, o_ref, *, n_ctx):
    cb = pl.program_id(1)
    x = x_ref[...]
    n = x.shape[0]
    w = w_ref[...]
    t = lax.broadcasted_iota(jnp.int32, (n, 1), 0)
    lo = jnp.where(t < n_ctx, 0, n_ctx)
    hi = jnp.where(t < n_ctx, n_ctx, n)
    acc = jnp.zeros_like(x)
    for j in range(DN_CONV):
        sft = j - (DN_CONV - 1) // 2
        xs = x if sft == 0 else pltpu.roll(x, (-sft) % n, 0)
        u = t + sft
        ok = (u >= lo) & (u < hi)
        acc = acc + jnp.where(ok, xs, 0.0) * w[j:j + 1, :]
    y = acc * (1.0 / (1.0 + jnp.exp(-acc)))
    nrm = lax.rsqrt(jnp.sum(y * y, axis=-1, keepdims=True) + EPS)
    scl = jnp.where(cb < DN_HK, nrm * (DN_DK ** -0.5), jnp.where(cb < 2 * DN_HK, nrm, 1.0))
    o_ref[...] = y * scl


def dn_conv(qkv_pre, conv_w, *, nb):
    t, n = qkv_pre.shape
    lt = t // nb
    return pl.pallas_call(
        functools.partial(_dn_conv_kernel, n_ctx=CTX_LEN),
        grid=(nb, n // LANES),
        in_specs=[pl.BlockSpec((lt, LANES), lambda b, c: (b, c)),
                  pl.BlockSpec((DN_CONV, LANES), lambda b, c: (0, c))],
        out_specs=pl.BlockSpec((lt, LANES), lambda b, c: (b, c)),
        out_shape=jax.ShapeDtypeStruct((t, n), F32),
        compiler_params=_cparams(2),
        name="dn_conv",
    )(qkv_pre, conv_w)


def _dn_gate_kernel(ba_ref, alog_ref, dtb_ref, beta_ref, gc_ref, gt_ref):
    n = ba_ref.shape[0]
    c = DN_CHUNK
    ba = ba_ref[...]
    beta_ref[...] = 1.0 / (1.0 + jnp.exp(-ba))
    z = ba + dtb_ref[...]
    softplus = jnp.maximum(z, 0.0) + jnp.log(1.0 + jnp.exp(-jnp.abs(z)))
    g = -jnp.exp(alog_ref[...]) * softplus
    row = lax.broadcasted_iota(jnp.int32, (c, c), 0)
    col = lax.broadcasted_iota(jnp.int32, (c, c), 1)
    prefix = (col <= row).astype(F32)
    suffix = (col >= row).astype(F32)
    ones = jnp.ones((c, c), F32)
    fwd_lane = lax.broadcasted_iota(jnp.int32, (1, LANES), 1) < 2 * DN_HV
    for ci in range(n // c):
        gch = g[ci * c:(ci + 1) * c, :]
        csum = jnp.where(fwd_lane,
                         jnp.dot(prefix, gch, preferred_element_type=F32, precision=HIGHEST),
                         jnp.dot(suffix, gch, preferred_element_type=F32, precision=HIGHEST))
        gc_ref[ci * c:(ci + 1) * c, :] = csum
        gt_ref[ci * c:(ci + 1) * c, :] = jnp.dot(ones, gch, preferred_element_type=F32,
                                                 precision=HIGHEST)


def dn_gates(ba, a_log, dt_bias, *, nb):
    t = ba.shape[0]
    lt = t // nb
    zeros = jnp.zeros((DN_HV,), F32)
    alog = jnp.concatenate([zeros, a_log[0], zeros, a_log[1]]).reshape(1, LANES)
    dtb = jnp.concatenate([zeros, dt_bias[0], zeros, dt_bias[1]]).reshape(1, LANES)
    blk = pl.BlockSpec((lt, LANES), lambda b: (b, 0))
    vec = pl.BlockSpec((1, LANES), lambda b: (0, 0))
    shp = jax.ShapeDtypeStruct((t, LANES), F32)
    return pl.pallas_call(
        _dn_gate_kernel,
        grid=(nb,),
        in_specs=[blk, vec, vec],
        out_specs=[blk, blk, blk],
        out_shape=[shp, shp, shp],
        compiler_params=_cparams(1),
        name="dn_gates",
    )(ba, alog, dtb)


def _dn_chunk_index(d, c, n_ctx_chunks, n_chunks):
    bwd = jnp.where(c < n_ctx_chunks, n_ctx_chunks - 1 - c, n_chunks - 1 + n_ctx_chunks - c)
    return jnp.where(d == 0, c, bwd)


def _dn_delta_kernel(q_ref, k_ref, v_ref, gcol_ref, grow_ref, o_ref, s_scr):
    d = pl.program_id(2)
    c = pl.program_id(3)
    cs = DN_CHUNK

    @pl.when(c == 0)
    def _():
        s_scr[...] = jnp.zeros_like(s_scr)

    row = lax.broadcasted_iota(jnp.int32, (cs, cs), 0)
    col = lax.broadcasted_iota(jnp.int32, (cs, cs), 1)
    ahead = (row - col) * jnp.where(d == 0, 1, -1)
    incl = ahead >= 0
    strict = ahead > 0
    eye = (row == col).astype(F32)

    for hk in range(DN_G // 2):
        qh = q_ref[:, hk * DN_DK:(hk + 1) * DN_DK]
        kh = k_ref[:, hk * DN_DK:(hk + 1) * DN_DK]
        kk = _mm_nt(kh, kh)
        qk = _mm_nt(qh, kh)
        for h in (2 * hk, 2 * hk + 1):
            vh = v_ref[:, h * DN_DK:(h + 1) * DN_DK]
            beta = gcol_ref[0, 0, 0, :, h:h + 1]
            gc = gcol_ref[0, 0, 0, :, DN_G + h:DN_G + h + 1]
            gt = gcol_ref[0, 0, 0, :, 2 * DN_G + h:2 * DN_G + h + 1]
            gr = grow_ref[0, 0, 0, 0, h:h + 1, :]
            decay = jnp.exp(jnp.where(incl, gc - gr, NEG_INF))
            x = -jnp.where(strict, kk * (beta * decay), 0.0)
            tinv = eye + x
            pw = x
            for _ in range(5):
                pw = _mm_split(pw, pw)
                tinv = tinv + _mm_split(tinv, pw)
            eg = jnp.exp(gc)
            uw = _mm(tinv, jnp.concatenate([vh * beta, kh * (beta * eg)], axis=1))
            u = uw[:, :DN_DK]
            w = uw[:, DN_DK:]
            s_h = s_scr[h]
            v_new = u - _mm(w, s_h)
            intra = jnp.where(incl, qk * decay, 0.0)
            o = _mm(qh * eg, s_h) + _mm(intra, v_new)
            o_ref[0, :, h * DN_DK:(h + 1) * DN_DK] = o
            kd = kh * jnp.exp(gt - gc)
            s_scr[h] = s_h * jnp.exp(gt[0:1, :]) + _mm(kd.T, v_new)


def dn_delta(qkv, gcol, grow, *, nb):
    t = qkv.shape[0]
    lt = t // nb
    nch = lt // DN_CHUNK
    nctx = CTX_LEN // DN_CHUNK
    ng = DN_HV // DN_G
    qw = DN_G // 2 * DN_DK
    vw = DN_G * DN_DK
    cidx = functools.partial(_dn_chunk_index, n_ctx_chunks=nctx, n_chunks=nch)
    return pl.pallas_call(
        _dn_delta_kernel,
        grid=(nb, ng, 2, nch),
        in_specs=[pl.BlockSpec((DN_CHUNK, qw), lambda b, g, d, c: (b * nch + cidx(d, c), g)),
                  pl.BlockSpec((DN_CHUNK, qw), lambda b, g, d, c: (b * nch + cidx(d, c),
                                                                   DN_QK // qw + g)),
                  pl.BlockSpec((DN_CHUNK, vw), lambda b, g, d, c: (b * nch + cidx(d, c),
                                                                   2 * DN_QK // vw + g)),
                  pl.BlockSpec((1, 1, 1, DN_CHUNK, 3 * DN_G),
                               lambda b, g, d, c: (b, d, g, cidx(d, c), 0)),
                  pl.BlockSpec((1, 1, 1, 1, DN_G, DN_CHUNK),
                               lambda b, g, d, c: (b, d, g, cidx(d, c), 0, 0))],
        out_specs=pl.BlockSpec((1, DN_CHUNK, vw), lambda b, g, d, c: (d, b * nch + cidx(d, c), g)),
        out_shape=jax.ShapeDtypeStruct((2, t, DN_VD), F32),
        scratch_shapes=[pltpu.VMEM((DN_G, DN_DK, DN_DK), F32)],
        compiler_params=_cparams(4),
        name="dn_delta",
    )(qkv, qkv, qkv, gcol, grow)


def _dn_gate_layouts(beta, gc, gt, *, nb):
    t = beta.shape[0]
    lt = t // nb
    nch = lt // DN_CHUNK
    ng = DN_HV // DN_G
    cols, rows = [], []
    for d in range(2):
        b0 = 2 * d * DN_HV
        parts = [beta[:, b0:b0 + DN_HV], gc[:, b0 + DN_HV:b0 + 2 * DN_HV],
                 gt[:, b0 + DN_HV:b0 + 2 * DN_HV]]
        parts = [p.reshape(nb, lt, ng, DN_G).transpose(0, 2, 1, 3) for p in parts]
        cols.append(jnp.concatenate(parts, axis=-1))
        rows.append(parts[1].reshape(nb, ng, nch, DN_CHUNK, DN_G).transpose(0, 1, 2, 4, 3))
    return jnp.stack(cols, axis=1), jnp.stack(rows, axis=1)


def _dn_out_kernel(o_ref, z_ref, nw_ref, w_ref, g_ref, s_ref, out_ref, a_scr):
    j = pl.program_id(1)

    @pl.when(j == 0)
    def _():
        for h in range(DN_HV):
            sl = slice(h * DN_DK, (h + 1) * DN_DK)
            o = o_ref[0, :, sl] + o_ref[1, :, sl]
            z = z_ref[:, sl]
            y = o * lax.rsqrt(jnp.mean(o * o, axis=-1, keepdims=True) + EPS) * nw_ref[...]
            a_scr[:, sl] = (y * (z * (1.0 / (1.0 + jnp.exp(-z))))).astype(BF16)

    acc = jnp.dot(a_scr[...], w_ref[...], preferred_element_type=F32)
    out_ref[...] = s_ref[...] + g_ref[0] * acc


def dn_out(o2, z, norm_w, w, mod, s2d, *, nb, gate_chunk, tn=512):
    t, d = s2d.shape
    k = o2.shape[2]
    tpb = t // nb // TM
    gspec = pl.BlockSpec((1, 1, tn), lambda r, j: (jnp.where(r % tpb == 0, nb, r // tpb), 0,
                                                   gate_chunk * (d // tn) + j))
    return pl.pallas_call(
        _dn_out_kernel,
        grid=(t // TM, d // tn),
        in_specs=[pl.BlockSpec((2, TM, k), lambda r, j: (0, r, 0)),
                  pl.BlockSpec((TM, k), lambda r, j: (r, 0)),
                  pl.BlockSpec((1, DN_DK), lambda r, j: (0, 0)),
                  pl.BlockSpec((k, tn), lambda r, j: (0, j)),
                  gspec,
                  pl.BlockSpec((TM, tn), lambda r, j: (r, j))],
        out_specs=pl.BlockSpec((TM, tn), lambda r, j: (r, j)),
        out_shape=jax.ShapeDtypeStruct((t, d), F32),
        scratch_shapes=[pltpu.VMEM((TM, k), BF16)],
        compiler_params=_cparams(2),
        name="dn_out",
    )(o2, z, norm_w.reshape(1, DN_DK), w, mod, s2d)


def _router_kernel(x_ref, nw_ref, sc_ref, sh_ref, wr_ref, br_ref, h_ref, eid_ref, gate_ref):
    h = _norm_mod(x_ref[...], nw_ref[...], sc_ref[0], sh_ref[0])
    h_ref[...] = h
    logits = jnp.dot(h, wr_ref[...], preferred_element_type=F32, precision=HIGHEST) + br_ref[...]
    lane = lax.broadcasted_iota(jnp.int32, logits.shape, 1)
    is_grp = lane < N_GROUPS
    mg = jnp.max(jnp.where(is_grp, logits, -jnp.inf), axis=-1, keepdims=True)
    eg = jnp.where(is_grp, jnp.exp(logits - mg), 0.0)
    pg = eg / jnp.sum(eg, axis=-1, keepdims=True)
    p_grp = jnp.max(pg, axis=-1, keepdims=True)
    g_sel = jnp.min(jnp.where(is_grp & (pg == p_grp), lane, LANES), axis=-1, keepdims=True)
    fine = lane - N_GROUPS
    in_grp = (fine >= 0) & (fine < N_EXPERTS) & ((fine // EXPERTS_PER_GROUP) == g_sel)
    mf = jnp.max(jnp.where(in_grp, logits, -jnp.inf), axis=-1, keepdims=True)
    ef = jnp.where(in_grp, jnp.exp(logits - mf), 0.0)
    pf = ef / jnp.sum(ef, axis=-1, keepdims=True)
    p1 = jnp.max(jnp.where(in_grp, pf, -1.0), axis=-1, keepdims=True)
    i1 = jnp.min(jnp.where(in_grp & (pf == p1), lane, LANES), axis=-1, keepdims=True)
    rest = in_grp & (lane != i1)
    p2 = jnp.max(jnp.where(rest, pf, -1.0), axis=-1, keepdims=True)
    i2 = jnp.min(jnp.where(rest & (pf == p2), lane, LANES), axis=-1, keepdims=True)
    den = p1 + p2
    eid_ref[...] = jnp.where(lane == 0, i1 - N_GROUPS, jnp.where(lane == 1, i2 - N_GROUPS, 0))
    gate_ref[...] = jnp.where(lane == 0, p_grp * p1 / den,
                              jnp.where(lane == 1, p_grp * p2 / den, 0.0))


def moe_router(s2d, nw, mod, wr, br, *, nb, sc_chunk, sh_chunk):
    t, d = s2d.shape
    tpb = t // nb // TM
    row = pl.BlockSpec((TM, d), lambda r: (r, 0))
    small = pl.BlockSpec((TM, LANES), lambda r: (r, 0))
    return pl.pallas_call(
        _router_kernel,
        grid=(t // TM,),
        in_specs=[row, pl.BlockSpec((1, d), lambda r: (0, 0)),
                  _mod_spec(nb, tpb, sc_chunk, d), _mod_spec(nb, tpb, sh_chunk, d),
                  pl.BlockSpec((d, LANES), lambda r: (0, 0)),
                  pl.BlockSpec((1, LANES), lambda r: (0, 0))],
        out_specs=[row, small, small],
        out_shape=[jax.ShapeDtypeStruct((t, d), F32),
                   jax.ShapeDtypeStruct((t, LANES), jnp.int32),
                   jax.ShapeDtypeStruct((t, LANES), F32)],
        compiler_params=_cparams(1),
        name="moe_router",
    )(s2d, nw.reshape(1, d), mod, mod, wr, br)


def _row_copy(src_hbm, row, dst, r, sem):
    return pltpu.make_async_copy(src_hbm.at[pl.ds(row, 1), :], dst.at[pl.ds(r, 1), :], sem)


def _moe_gather_kernel(tok_ref, nused_ref, h_hbm, o_ref, buf, sem):
    blk = pl.program_id(0)
    rows = buf.shape[0]

    @pl.when(blk < nused_ref[0])
    def _():
        def issue(r, carry):
            _row_copy(h_hbm, tok_ref[blk * rows + r], buf, r, sem).start()
            return carry
        lax.fori_loop(0, rows, issue, 0, unroll=8)

        def wait(r, carry):
            _row_copy(h_hbm, 0, buf, r, sem).wait()
            return carry
        lax.fori_loop(0, rows, wait, 0, unroll=8)
        o_ref[...] = buf[...].astype(BF16)

    @pl.when(blk >= nused_ref[0])
    def _():
        o_ref[...] = jnp.zeros_like(o_ref)


def moe_gather(h, buf_tok, n_used):
    t, d = h.shape
    p = buf_tok.shape[0]
    return pl.pallas_call(
        _moe_gather_kernel,
        grid_spec=pltpu.PrefetchScalarGridSpec(
            num_scalar_prefetch=2,
            grid=(p // MOE_ROWS,),
            in_specs=[pl.BlockSpec(memory_space=pl.ANY)],
            out_specs=pl.BlockSpec((MOE_ROWS, d), lambda i, *_: (i, 0)),
            scratch_shapes=[pltpu.VMEM((MOE_ROWS, d), F32), pltpu.SemaphoreType.DMA(())]),
        out_shape=jax.ShapeDtypeStruct((p, d), BF16),
        compiler_params=_cparams(1),
        name="moe_gather",
    )(buf_tok, n_used, h)


def _moe_expert_kernel(be_ref, nused_ref, x_ref, w13_ref, w2_ref, o_ref):
    blk = pl.program_id(0)

    @pl.when(blk < nused_ref[0])
    def _():
        h13 = jnp.dot(x_ref[...], w13_ref[0], preferred_element_type=F32)
        gate = h13[:, :D_EXPERT]
        act = gate * (1.0 / (1.0 + jnp.exp(-gate))) * h13[:, D_EXPERT:]
        o_ref[...] = jnp.dot(act.astype(BF16), w2_ref[0], preferred_element_type=F32)

    @pl.when(blk >= nused_ref[0])
    def _():
        o_ref[...] = jnp.zeros_like(o_ref)


def moe_experts(xs, block_e, n_used, w13, w2):
    p, d = xs.shape
    return pl.pallas_call(
        _moe_expert_kernel,
        grid_spec=pltpu.PrefetchScalarGridSpec(
            num_scalar_prefetch=2,
            grid=(p // MOE_ROWS,),
            in_specs=[pl.BlockSpec((MOE_ROWS, d), lambda i, be, nu: (i, 0)),
                      pl.BlockSpec((1, d, 2 * D_EXPERT), lambda i, be, nu: (be[i], 0, 0)),
                      pl.BlockSpec((1, D_EXPERT, d), lambda i, be, nu: (be[i], 0, 0))],
            out_specs=pl.BlockSpec((MOE_ROWS, d), lambda i, be, nu: (i, 0))),
        out_shape=jax.ShapeDtypeStruct((p, d), F32),
        compiler_params=_cparams(1),
        name="moe_experts",
    )(block_e, n_used, xs, w13, w2)


def _moe_combine_kernel(d0_ref, d1_ref, y_hbm, s_ref, gate_ref, g_ref, o_ref, buf0, buf1, sem):
    i = pl.program_id(0)
    rows = buf0.shape[0]

    def issue(r, carry):
        _row_copy(y_hbm, d0_ref[i * rows + r], buf0, r, sem.at[0]).start()
        _row_copy(y_hbm, d1_ref[i * rows + r], buf1, r, sem.at[1]).start()
        return carry
    lax.fori_loop(0, rows, issue, 0, unroll=8)

    def wait(r, carry):
        _row_copy(y_hbm, 0, buf0, r, sem.at[0]).wait()
        _row_copy(y_hbm, 0, buf1, r, sem.at[1]).wait()
        return carry
    lax.fori_loop(0, rows, wait, 0, unroll=8)
    gates = gate_ref[...]
    y = gates[:, 0:1] * buf0[...] + gates[:, 1:2] * buf1[...]
    o_ref[...] = s_ref[...] + g_ref[0] * y


def moe_combine(yb, dest0, dest1, s2d, gates, mod, *, nb, gate_chunk):
    t, d = s2d.shape
    tpb = t // nb // TM
    return pl.pallas_call(
        _moe_combine_kernel,
        grid_spec=pltpu.PrefetchScalarGridSpec(
            num_scalar_prefetch=2,
            grid=(t // TM,),
            in_specs=[pl.BlockSpec(memory_space=pl.ANY),
                      pl.BlockSpec((TM, d), lambda r, *_: (r, 0)),
                      pl.BlockSpec((TM, LANES), lambda r, *_: (r, 0)),
                      _mod_spec(nb, tpb, gate_chunk, d)],
            out_specs=pl.BlockSpec((TM, d), lambda r, *_: (r, 0)),
            scratch_shapes=[pltpu.VMEM((TM, d), F32), pltpu.VMEM((TM, d), F32),
                            pltpu.SemaphoreType.DMA((2,))]),
        out_shape=jax.ShapeDtypeStruct((t, d), F32),
        compiler_params=_cparams(1),
        name="moe_combine",
    )(dest0, dest1, yb, s2d, gates, mod)


def _moe_plan(eid):
    t = eid.shape[0]
    tk = t * TOP_K
    e_flat = eid[:, :TOP_K].reshape(tk)
    onehot = (e_flat[:, None] == jnp.arange(N_EXPERTS, dtype=jnp.int32)[None, :]).astype(jnp.int32)
    csum = jnp.cumsum(onehot, axis=0)
    rank = jnp.take_along_axis(csum, e_flat[:, None], axis=1)[:, 0] - 1
    counts = csum[-1]
    padded = (counts + MOE_ROWS - 1) // MOE_ROWS * MOE_ROWS
    pend = jnp.cumsum(padded)
    dest = (pend - padded)[e_flat] + rank
    n_blocks = tk // MOE_ROWS + N_EXPERTS
    buf_tok = jnp.zeros((n_blocks * MOE_ROWS,), jnp.int32).at[dest].set(
        jnp.arange(tk, dtype=jnp.int32) // TOP_K)
    block_e = jnp.searchsorted(pend, jnp.arange(n_blocks, dtype=jnp.int32) * MOE_ROWS, side='right')
    block_e = jnp.minimum(block_e, N_EXPERTS - 1).astype(jnp.int32)
    n_used = (pend[-1:] // MOE_ROWS).astype(jnp.int32)
    dest = dest.reshape(t, TOP_K)
    return buf_tok, block_e, n_used, dest[:, 0], dest[:, 1]


def hier_moe(s2d, nw, mod, wg, bg, we, be, w13, w2, *, nb):
    d = s2d.shape[1]
    pad = LANES - N_GROUPS - N_EXPERTS
    wr = jnp.concatenate([wg, we, jnp.zeros((d, pad), F32)], axis=1)
    br = jnp.concatenate([bg, be, jnp.zeros((pad,), F32)]).reshape(1, LANES)
    h, eid, gates = moe_router(s2d, nw, mod, wr, br, nb=nb, sc_chunk=4, sh_chunk=3)
    buf_tok, block_e, n_used, dest0, dest1 = _moe_plan(eid)
    xs = moe_gather(h, buf_tok, n_used)
    yb = moe_experts(xs, block_e, n_used, w13.astype(BF16), w2.astype(BF16))
    return moe_combine(yb, dest0, dest1, s2d, gates, mod, nb=nb, gate_chunk=5)


def _final_norm_kernel(x_ref, w_ref, o_ref):
    x = x_ref[...]
    o_ref[0] = x * lax.rsqrt(jnp.mean(x * x, axis=-1, keepdims=True) + EPS) * w_ref[...]


def final_norm(s2d, w, *, nb):
    t, d = s2d.shape
    lt = t // nb
    tpb = lt // TM
    ctx_tiles = CTX_LEN // TM
    return pl.pallas_call(
        _final_norm_kernel,
        grid=(nb, tpb - ctx_tiles),
        in_specs=[pl.BlockSpec((TM, d), lambda b, i: (b * tpb + ctx_tiles + i, 0)),
                  pl.BlockSpec((1, d), lambda b, i: (0, 0))],
        out_specs=pl.BlockSpec((1, TM, d), lambda b, i: (b, i, 0)),
        out_shape=jax.ShapeDtypeStruct((nb, lt - CTX_LEN, d), F32),
        compiler_params=_cparams(2),
        name="final_norm",
    )(s2d, w.reshape(1, d))


def _rope_tables(seq, width):
    dim = DA_DIM
    half = dim // 2
    rows = seq // GRID_W
    row, col = jnp.meshgrid(jnp.arange(rows), jnp.arange(GRID_W), indexing='ij')
    row = row.reshape(-1).astype(F32)
    col = col.reshape(-1).astype(F32)
    inv_freq = 1.0 / (ROPE_BASE ** (jnp.arange(0, half, 2, dtype=F32) / half))

    def table(pos):
        ang = pos[:, None] * inv_freq[None, :]
        ang = jnp.concatenate([ang, ang], axis=-1)
        return jnp.cos(ang), jnp.sin(ang)
    cr, sr = table(row)
    cc, scol = table(col)
    cos = jnp.concatenate([cr, cc], -1)
    sin = jnp.concatenate([sr, scol], -1)
    quarter = (jnp.arange(dim) // (half // 2)) % 2
    sa = jnp.where(quarter[None, :] == 0, -sin, 0.0)
    sb = jnp.where(quarter[None, :] == 1, sin, 0.0)
    ctx1 = jnp.ones((CTX_LEN, dim), F32)
    ctx0 = jnp.zeros((CTX_LEN, dim), F32)
    reps = width // dim
    return tuple(jnp.tile(jnp.concatenate([c, x], axis=0), (1, reps))
                 for c, x in ((ctx1, cos), (ctx0, sa), (ctx0, sb)))


def kernel(x, c, ctx, c_ctx, ada_w, ada_b, norm_mix_w, norm_ffn_w, da_w_qkv, da_lambda, da_subln_w, da_w_o, dn_w_in, dn_conv_w, dn_a_log, dn_dt_bias, dn_norm_w, dn_w_o, wa_w_qkv, wa_sinks, wa_w_o, moe_wg, moe_bg, moe_we, moe_be, moe_w13, moe_w2, final_norm_w):
    nb, seq, d = x.shape
    lt = CTX_LEN + seq
    t = nb * lt
    s2d = jnp.concatenate([ctx, x], axis=1).reshape(t, d)
    cc = jnp.concatenate([c, c_ctx[None, :], jnp.zeros((16 - nb - 1, d), F32)], axis=0)
    mods = ada_tables(cc, ada_w, ada_b).reshape(DEPTH, 16, 1, 6 * d)
    rope_da = _rope_tables(seq, 512)
    rope_wa = _rope_tables(seq, 256)
    counts = [0] * N_MIXERS
    for i in range(DEPTH):
        kind = i % N_MIXERS
        j = counts[kind]
        counts[kind] += 1
        mod = mods[i]
        common = dict(nb=nb, sc_chunk=1, sh_chunk=0)
        if kind == 0:
            lambda_init = 0.8 - 0.6 * math.exp(-0.3 * i)
            qkv = proj(s2d, norm_mix_w[i], mod, da_w_qkv[j].astype(BF16), tn=512, out_dtype=BF16,
                       rope=rope_da, rope_tiles=2 * d // 512, **common)
            a = diff_attention(qkv, da_lambda[j], da_subln_w[j], nb=nb, lambda_init=lambda_init)
            s2d = out_proj(a, da_w_o[j].astype(BF16), mod, s2d, nb=nb, gate_chunk=2)
        elif kind == 1:
            w_in = dn_w_in[j]
            qkv_pre = proj(s2d, norm_mix_w[i], mod, w_in[:, :DN_QKV].astype(BF16), tn=1024,
                           out_dtype=F32, **common)
            z = proj(s2d, norm_mix_w[i], mod, w_in[:, DN_QKV:DN_QKV + DN_VD].astype(BF16),
                     tn=1024, out_dtype=F32, **common)
            ba = proj(s2d, norm_mix_w[i], mod, w_in[:, DN_QKV + DN_VD:], tn=LANES,
                      out_dtype=F32, **common)
            qkv = dn_conv(qkv_pre, dn_conv_w[j], nb=nb)
            beta, gc, gt = dn_gates(ba, dn_a_log[j], dn_dt_bias[j], nb=nb)
            gcol, grow = _dn_gate_layouts(beta, gc, gt, nb=nb)
            o2 = dn_delta(qkv, gcol, grow, nb=nb)
            s2d = dn_out(o2, z, dn_norm_w[j], dn_w_o[j].astype(BF16), mod, s2d, nb=nb, gate_chunk=2)
        else:
            qkv = proj(s2d, norm_mix_w[i], mod, wa_w_qkv[j].astype(BF16), tn=256, out_dtype=BF16,
                       rope=rope_wa, rope_tiles=(WA_HQ + WA_HKV) * WA_DIM // 256, **common)
            a = window_attention(qkv, wa_sinks[j], nb=nb)
            s2d = out_proj(a, wa_w_o[j].astype(BF16), mod, s2d, nb=nb, gate_chunk=2)
        s2d = hier_moe(s2d, norm_ffn_w[i], mod, moe_wg[i], moe_bg[i], moe_we[i], moe_be[i],
                       moe_w13[i], moe_w2[i], nb=nb)
    return final_norm(s2d, final_norm_w, nb=nb)
```

```python
import functools
import math

import jax
import jax.numpy as jnp
from jax import lax
from jax.experimental import pallas as pl
from jax.experimental.pallas import tpu as pltpu

F32 = jnp.float32
BF16 = jnp.bfloat16
HIGHEST = lax.Precision.HIGHEST

D_MODEL = 2048
DEPTH = 4
GRID_W = 64
CTX_LEN = 256
N_MIXERS = 3
EPS = 1e-6
ROPE_BASE = 10000.0
NEG_INF = -1e30
DA_HEADS = 16
DA_DIM = 64
DN_HK = 16
DN_HV = 32
DN_DK = 128
DN_CONV = 5
DN_CHUNK = 64
DN_QK = DN_HK * DN_DK
DN_VD = DN_HV * DN_DK
DN_QKV = 2 * DN_QK + DN_VD
WA_HQ = 32
WA_HKV = 4
WA_GROUP = 8
WA_DIM = 64
WINDOW = 128
N_GROUPS = 4
EXPERTS_PER_GROUP = 8
N_EXPERTS = 32
TOP_K = 2
D_EXPERT = 768

LANES = 128
TM = 256
BIG_TILES_PER_BATCH = 3
MOE_ROWS = 256
DN_G = 8
DA_TQ = 512
VMEM_LIMIT = 56 * 1024 * 1024


def _cparams(n_axes):
    return pltpu.CompilerParams(dimension_semantics=("arbitrary",) * n_axes,
                                vmem_limit_bytes=VMEM_LIMIT)


def _mm(a, b):
    return jnp.dot(a.astype(BF16), b.astype(BF16), preferred_element_type=F32)


def _mm_nt(a, b):
    return lax.dot_general(a.astype(BF16), b.astype(BF16), (((1,), (1,)), ((), ())),
                           preferred_element_type=F32)


def _split_bf16(a):
    hi = a.astype(BF16)
    return hi, (a - hi.astype(F32)).astype(BF16)


def _mm_presplit(a_parts, b_parts):
    b_hi, b_lo = b_parts
    m = a_parts[0].shape[0]
    lhs = jnp.concatenate(a_parts, axis=0)
    o_hi = jnp.dot(lhs, b_hi, preferred_element_type=F32)
    o_lo = jnp.dot(lhs, b_lo, preferred_element_type=F32)
    return (o_hi[:m] + o_hi[m:]) + (o_lo[:m] + o_lo[m:])


def _sigmoid(x):
    return 1.0 / (1.0 + jnp.exp(-x))


def _mod_spec(nb, tpb, chunk, width):
    def imap(r, *_):
        return (jnp.where(r % tpb == 0, nb, r // tpb), 0, chunk)
    return pl.BlockSpec((1, 1, width), imap)


def _big_mod_specs(nb, tpb, chunk, width, col=None):
    def blk(j):
        return chunk if col is None else chunk * col + j
    return [pl.BlockSpec((1, 1, width), lambda r, j: (nb, 0, blk(j))),
            pl.BlockSpec((1, 1, width), lambda r, j: (r // tpb, 0, blk(j)))]


def _ctx_rows(rows, tiles_per_batch):
    first = (pl.program_id(0) % tiles_per_batch) * rows
    return first + lax.broadcasted_iota(jnp.int32, (rows, 1), 0) < CTX_LEN


def _ada_kernel(c_ref, w_ref, b_ref, o_ref):
    c = c_ref[...]
    o_ref[0] = jnp.dot(c * _sigmoid(c), w_ref[0], preferred_element_type=F32,
                       precision=HIGHEST) + b_ref[0]


def ada_tables(cc, ada_w, ada_b):
    depth, d, n = ada_w.shape
    tn = 768
    return pl.pallas_call(
        _ada_kernel,
        grid=(depth, n // tn),
        in_specs=[pl.BlockSpec((16, d), lambda l, j: (0, 0)),
                  pl.BlockSpec((1, d, tn), lambda l, j: (l, 0, j)),
                  pl.BlockSpec((1, 1, tn), lambda l, j: (l, 0, j))],
        out_specs=pl.BlockSpec((1, 16, tn), lambda l, j: (l, 0, j)),
        out_shape=jax.ShapeDtypeStruct((depth, 16, n), F32),
        compiler_params=_cparams(2),
        name="ada_tables",
    )(cc, ada_w, ada_b.reshape(depth, 1, n))


def _norm_mod(x, nw, sc, sh):
    y = x * lax.rsqrt(jnp.mean(x * x, axis=-1, keepdims=True) + EPS) * nw
    return y * (1.0 + sc) + sh


def _proj_kernel(*refs, epilogue, rope_tiles, tpb):
    if epilogue == "rope":
        (x_ref, nw_ref, scc_ref, scx_ref, shc_ref, shx_ref, w_ref, cos_ref, sa_ref, sb_ref,
         o_ref, h_scr) = refs
    else:
        x_ref, nw_ref, scc_ref, scx_ref, shc_ref, shx_ref, w_ref, o_ref, h_scr = refs
    j = pl.program_id(1)

    @pl.when(j == 0)
    def _():
        is_ctx = _ctx_rows(x_ref.shape[0], tpb)
        sc = jnp.where(is_ctx, scc_ref[0], scx_ref[0])
        sh = jnp.where(is_ctx, shc_ref[0], shx_ref[0])
        h_scr[...] = _norm_mod(x_ref[...], nw_ref[...], sc, sh).astype(h_scr.dtype)

    precision = HIGHEST if w_ref.dtype == F32 else None
    acc = jnp.dot(h_scr[...], w_ref[...], preferred_element_type=F32, precision=precision)
    if epilogue == "rope":
        @pl.when(j < rope_tiles)
        def _():
            cos, sa, sb = cos_ref[...], sa_ref[...], sb_ref[...]
            for g in range(acc.shape[1] // LANES):
                a = acc[:, g * LANES:(g + 1) * LANES]
                rot = a * cos + pltpu.roll(a, LANES - 16, 1) * sa + pltpu.roll(a, 16, 1) * sb
                o_ref[:, g * LANES:(g + 1) * LANES] = rot.astype(o_ref.dtype)

        @pl.when(j >= rope_tiles)
        def _():
            o_ref[...] = acc.astype(o_ref.dtype)
    elif epilogue == "silu":
        o_ref[...] = (acc * _sigmoid(acc)).astype(o_ref.dtype)
    else:
        o_ref[...] = acc.astype(o_ref.dtype)


def proj(s2d, nw, mod, w, *, nb, tn, out_dtype, epilogue=None, rope=None, rope_tiles=0):
    t, d = s2d.shape
    n = w.shape[1]
    tpb = BIG_TILES_PER_BATCH
    tm = t // nb // tpb
    in_specs = ([pl.BlockSpec((tm, d), lambda r, j: (r, 0)),
                 pl.BlockSpec((1, d), lambda r, j: (0, 0))]
                + _big_mod_specs(nb, tpb, 1, d) + _big_mod_specs(nb, tpb, 0, d)
                + [pl.BlockSpec((d, tn), lambda r, j: (0, j))])
    args = [s2d, nw.reshape(1, d), mod, mod, mod, mod, w]
    if epilogue == "rope":
        tab = pl.BlockSpec((tm, LANES), lambda r, j: (r % tpb, 0))
        in_specs += [tab, tab, tab]
        args += list(rope)
    return pl.pallas_call(
        functools.partial(_proj_kernel, epilogue=epilogue, rope_tiles=rope_tiles, tpb=tpb),
        grid=(t // tm, n // tn),
        in_specs=in_specs,
        out_specs=pl.BlockSpec((tm, tn), lambda r, j: (r, j)),
        out_shape=jax.ShapeDtypeStruct((t, n), out_dtype),
        scratch_shapes=[pltpu.VMEM((tm, d), w.dtype)],
        compiler_params=_cparams(2),
        name="proj",
    )(*args)


def _out_kernel(a_ref, w_ref, gc_ref, gx_ref, s_ref, o_ref, *, tpb):
    acc = jnp.dot(a_ref[...], w_ref[...], preferred_element_type=F32)
    gate = jnp.where(_ctx_rows(a_ref.shape[0], tpb), gc_ref[0], gx_ref[0])
    o_ref[...] = s_ref[...] + gate * acc


def out_proj(a, w, mod, s2d, *, nb, gate_chunk, tn=512):
    t, k = a.shape
    d = w.shape[1]
    tpb = BIG_TILES_PER_BATCH
    tm = t // nb // tpb
    return pl.pallas_call(
        functools.partial(_out_kernel, tpb=tpb),
        grid=(t // tm, d // tn),
        in_specs=[pl.BlockSpec((tm, k), lambda r, j: (r, 0)),
                  pl.BlockSpec((k, tn), lambda r, j: (0, j))]
                 + _big_mod_specs(nb, tpb, gate_chunk, tn, col=d // tn)
                 + [pl.BlockSpec((tm, tn), lambda r, j: (r, j))],
        out_specs=pl.BlockSpec((tm, tn), lambda r, j: (r, j)),
        out_shape=jax.ShapeDtypeStruct((t, d), F32),
        compiler_params=_cparams(2),
        name="out_proj",
    )(a, w, mod, mod, s2d)


def _da_kernel(lam_ref, q_ref, k_ref, v_ref, sw_ref, o_ref, klo, khi, *, lambda_init, n_ctx, tq):
    lam = lam_ref[...]
    l01 = jnp.sum(lam[0:1] * lam[1:2], axis=-1, keepdims=True)
    l23 = jnp.sum(lam[2:3] * lam[3:4], axis=-1, keepdims=True)
    lam_full = jnp.exp(l01) - jnp.exp(l23) + lambda_init
    lane = lax.broadcasted_iota(jnp.int32, (1, LANES), 1)
    k = k_ref[...]
    zero = jnp.zeros_like(k)
    klo[...] = jnp.where(lane < DA_DIM, k, zero)
    khi[...] = jnp.where(lane >= DA_DIM, k, zero)

    def attend(row0, rows, nk):
        q = q_ref[pl.ds(row0, rows), :] * (DA_DIM ** -0.5)
        v = v_ref[0:nk, :]
        s1 = _mm_nt(q, klo[0:nk, :])
        s2 = _mm_nt(q, khi[0:nk, :])
        e1 = jnp.exp(s1 - jnp.max(s1, axis=-1, keepdims=True))
        e2 = jnp.exp(s2 - jnp.max(s2, axis=-1, keepdims=True))
        o1 = _mm(e1, v) / jnp.sum(e1, axis=-1, keepdims=True)
        o2 = _mm(e2, v) / jnp.sum(e2, axis=-1, keepdims=True)
        o = o1 - lam_full * o2
        o = o * lax.rsqrt(jnp.mean(o * o, axis=-1, keepdims=True) + EPS) * sw_ref[...]
        o_ref[pl.ds(row0, rows), :] = (o * (1.0 - lambda_init)).astype(o_ref.dtype)

    attend(0, n_ctx, n_ctx)

    def latent_tile(i, carry):
        attend(pl.multiple_of(n_ctx + i * tq, LANES), tq, k_ref.shape[0])
        return carry
    lax.fori_loop(0, (q_ref.shape[0] - n_ctx) // tq, latent_tile, 0)


def diff_attention(qkv, lam, subln_w, *, nb, lambda_init):
    t = qkv.shape[0]
    lt = t // nb
    col = pl.BlockSpec((lt, LANES), lambda b, h: (b, h))
    return pl.pallas_call(
        functools.partial(_da_kernel, lambda_init=lambda_init, n_ctx=CTX_LEN, tq=DA_TQ),
        grid=(nb, DA_HEADS),
        in_specs=[pl.BlockSpec((4, DA_DIM), lambda b, h: (0, 0)),
                  col,
                  pl.BlockSpec((lt, LANES), lambda b, h: (b, DA_HEADS + h)),
                  pl.BlockSpec((lt, LANES), lambda b, h: (b, 2 * DA_HEADS + h)),
                  pl.BlockSpec((1, LANES), lambda b, h: (0, 0))],
        out_specs=col,
        out_shape=jax.ShapeDtypeStruct((t, D_MODEL), BF16),
        scratch_shapes=[pltpu.VMEM((lt, LANES), BF16)] * 2,
        compiler_params=_cparams(2),
        name="diff_attention",
    )(lam, qkv, qkv, qkv, subln_w.reshape(1, LANES))


def _wa_kernel(sink_ref, q_ref, k_ref, v_ref, o_ref, klo, khi, vlo, vhi, *, n_ctx, seq):
    hkv = pl.program_id(1)
    i = pl.program_id(2)
    tq = q_ref.shape[0]
    band = tq + 2 * WINDOW
    npair = WA_GROUP // 2
    lane = lax.broadcasted_iota(jnp.int32, (1, LANES), 1)
    n_ctx_tiles = n_ctx // tq

    @pl.when(i == 0)
    def _():
        odd = (hkv % 2) == 1
        for src, lo, hi in ((k_ref, klo, khi), (v_ref, vlo, vhi)):
            x = src[...].astype(F32)
            x = jnp.where(odd, pltpu.roll(x, WA_DIM, 1), x)
            x = jnp.where(lane < WA_DIM, x, 0.0)
            lo[...] = x.astype(BF16)
            hi[...] = pltpu.roll(x, WA_DIM, 1).astype(BF16)

    def attend(with_band):
        q = jnp.concatenate([q_ref[:, p * LANES:(p + 1) * LANES] for p in range(npair)],
                            axis=0) * (WA_DIM ** -0.5)
        rows = npair * tq
        if with_band:
            lat0 = (i - n_ctx_tiles) * tq
            start = jnp.clip(lat0 - WINDOW, 0, seq - band)
            row0 = pl.multiple_of(n_ctx + start, LANES)
            qpos = lat0 + (lax.broadcasted_iota(jnp.int32, (rows, band), 0) & (tq - 1))
            kpos = start + lax.broadcasted_iota(jnp.int32, (rows, band), 1)
            valid = jnp.abs(qpos - kpos) <= WINDOW
        out = None
        for par, (kx, vx) in enumerate(((klo, vlo), (khi, vhi))):
            sink = jnp.concatenate([jnp.full((tq, 1), sink_ref[hkv, 2 * p + par], F32)
                                    for p in range(npair)], axis=0)
            s_c = _mm_nt(q, kx[0:n_ctx, :])
            m = jnp.maximum(jnp.max(s_c, axis=-1, keepdims=True), sink)
            if with_band:
                s_b = jnp.where(valid, _mm_nt(q, kx[pl.ds(row0, band), :]), NEG_INF)
                m = jnp.maximum(m, jnp.max(s_b, axis=-1, keepdims=True))
            e_c = jnp.exp(s_c - m)
            den = jnp.sum(e_c, axis=-1, keepdims=True) + jnp.exp(sink - m)
            acc = _mm(e_c, vx[0:n_ctx, :])
            if with_band:
                e_b = jnp.exp(s_b - m)
                den = den + jnp.sum(e_b, axis=-1, keepdims=True)
                acc = acc + _mm(e_b, vx[pl.ds(row0, band), :])
            out = acc / den if out is None else out + acc / den
        for p in range(npair):
            o_ref[:, p * LANES:(p + 1) * LANES] = out[p * tq:(p + 1) * tq].astype(o_ref.dtype)

    @pl.when(i < n_ctx_tiles)
    def _():
        attend(False)

    @pl.when(i >= n_ctx_tiles)
    def _():
        attend(True)


def window_attention(qkv, sinks, *, nb):
    t = qkv.shape[0]
    lt = t // nb
    tq = 128
    nq = lt // tq
    qw = WA_GROUP * WA_DIM
    kblk = WA_HQ * WA_DIM // LANES
    vblk = kblk + WA_HKV * WA_DIM // LANES
    return pl.pallas_call(
        functools.partial(_wa_kernel, n_ctx=CTX_LEN, seq=lt - CTX_LEN),
        grid=(nb, WA_HKV, nq),
        in_specs=[pl.BlockSpec(memory_space=pltpu.SMEM),
                  pl.BlockSpec((tq, qw), lambda b, h, i: (b * nq + i, h)),
                  pl.BlockSpec((lt, LANES), lambda b, h, i: (b, kblk + h // 2)),
                  pl.BlockSpec((lt, LANES), lambda b, h, i: (b, vblk + h // 2))],
        out_specs=pl.BlockSpec((tq, qw), lambda b, h, i: (b * nq + i, h)),
        out_shape=jax.ShapeDtypeStruct((t, D_MODEL), BF16),
        scratch_shapes=[pltpu.VMEM((lt, LANES), BF16)] * 4,
        compiler_params=_cparams(3),
        name="window_attention",
    )(sinks.reshape(WA_HKV, WA_GROUP), qkv, qkv, qkv)


def _dn_conv_kernel(x_ref, w_ref, o_ref, *, n_ctx):
    cb = pl.program_id(1)
    x = x_ref[...]
    n = x.shape[0]
    w = w_ref[...]
    t = lax.broadcasted_iota(jnp.int32, (n, 1), 0)
    lo = jnp.where(t < n_ctx, 0, n_ctx)
    hi = jnp.where(t < n_ctx, n_ctx, n)
    acc = jnp.zeros_like(x)
    for j in range(DN_CONV):
        sft = j - (DN_CONV - 1) // 2
        xs = x if sft == 0 else pltpu.roll(x, (-sft) % n, 0)
        u = t + sft
        ok = (u >= lo) & (u < hi)
        acc = acc + jnp.where(ok, xs, 0.0) * w[j:j + 1, :]
    y = acc * _sigmoid(acc)
    nrm = lax.rsqrt(jnp.sum(y * y, axis=-1, keepdims=True) + EPS)
    scl = jnp.where(cb < DN_HK, nrm * (DN_DK ** -0.5), jnp.where(cb < 2 * DN_HK, nrm, 1.0))
    o_ref[...] = y * scl


def dn_conv(qkv_pre, conv_w, *, nb):
    t, n = qkv_pre.shape
    lt = t // nb
    return pl.pallas_call(
        functools.partial(_dn_conv_kernel, n_ctx=CTX_LEN),
        grid=(nb, n // LANES),
        in_specs=[pl.BlockSpec((lt, LANES), lambda b, c: (b, c)),
                  pl.BlockSpec((DN_CONV, LANES), lambda b, c: (0, c))],
        out_specs=pl.BlockSpec((lt, LANES), lambda b, c: (b, c)),
        out_shape=jax.ShapeDtypeStruct((t, n), F32),
        compiler_params=_cparams(2),
        name="dn_conv",
    )(qkv_pre, conv_w)


def _dn_gate_kernel(ba_ref, alog_ref, dtb_ref, beta_ref, gc_ref, gt_ref):
    n = ba_ref.shape[0]
    c = DN_CHUNK
    ba = ba_ref[...]
    beta_ref[...] = _sigmoid(ba)
    z = ba + dtb_ref[...]
    softplus = jnp.maximum(z, 0.0) + jnp.log(1.0 + jnp.exp(-jnp.abs(z)))
    g = -jnp.exp(alog_ref[...]) * softplus
    row = lax.broadcasted_iota(jnp.int32, (c, c), 0)
    col = lax.broadcasted_iota(jnp.int32, (c, c), 1)
    prefix = (col <= row).astype(F32)
    suffix = (col >= row).astype(F32)
    ones = jnp.ones((c, c), F32)
    fwd_lane = lax.broadcasted_iota(jnp.int32, (1, LANES), 1) < 2 * DN_HV
    for ci in range(n // c):
        gch = g[ci * c:(ci + 1) * c, :]
        csum = jnp.where(fwd_lane,
                         jnp.dot(prefix, gch, preferred_element_type=F32, precision=HIGHEST),
                         jnp.dot(suffix, gch, preferred_element_type=F32, precision=HIGHEST))
        gc_ref[ci * c:(ci + 1) * c, :] = csum
        gt_ref[ci * c:(ci + 1) * c, :] = jnp.dot(ones, gch, preferred_element_type=F32,
                                                 precision=HIGHEST)


def dn_gates(ba, a_log, dt_bias, *, nb):
    t = ba.shape[0]
    lt = t // nb
    zeros = jnp.zeros((DN_HV,), F32)
    alog = jnp.concatenate([zeros, a_log[0], zeros, a_log[1]]).reshape(1, LANES)
    dtb = jnp.concatenate([zeros, dt_bias[0], zeros, dt_bias[1]]).reshape(1, LANES)
    blk = pl.BlockSpec((lt, LANES), lambda b: (b, 0))
    vec = pl.BlockSpec((1, LANES), lambda b: (0, 0))
    shp = jax.ShapeDtypeStruct((t, LANES), F32)
    return pl.pallas_call(
        _dn_gate_kernel,
        grid=(nb,),
        in_specs=[blk, vec, vec],
        out_specs=[blk, blk, blk],
        out_shape=[shp, shp, shp],
        compiler_params=_cparams(1),
        name="dn_gates",
    )(ba, alog, dtb)


def _dn_chunk_index(d, c, n_ctx_chunks, n_chunks):
    bwd = jnp.where(c < n_ctx_chunks, n_ctx_chunks - 1 - c, n_chunks - 1 + n_ctx_chunks - c)
    return jnp.where(d == 0, c, bwd)


def _dn_delta_kernel(q_ref, k_ref, v_ref, gcol_ref, grow_ref, o_ref, s_scr):
    d = pl.program_id(2)
    c = pl.program_id(3)
    cs = DN_CHUNK

    @pl.when(c == 0)
    def _():
        s_scr[...] = jnp.zeros_like(s_scr)

    row = lax.broadcasted_iota(jnp.int32, (cs, cs), 0)
    col = lax.broadcasted_iota(jnp.int32, (cs, cs), 1)
    ahead = (row - col) * jnp.where(d == 0, 1, -1)
    incl = ahead >= 0
    strict = ahead > 0
    eye = (row == col).astype(F32)

    gates = gcol_ref[0, 0, 0]
    grows = grow_ref[0, 0, 0, 0]
    heads = range(DN_G)
    qs = [q_ref[:, hk * DN_DK:(hk + 1) * DN_DK] for hk in range(DN_G // 2)]
    ks = [k_ref[:, hk * DN_DK:(hk + 1) * DN_DK] for hk in range(DN_G // 2)]
    vs = [v_ref[:, h * DN_DK:(h + 1) * DN_DK] for h in heads]
    kks = [_mm_nt(k, k) for k in ks]
    qks = [_mm_nt(q, k) for q, k in zip(qs, ks)]
    beta = [gates[:, h:h + 1] for h in heads]
    gc = [gates[:, DN_G + h:DN_G + h + 1] for h in heads]
    gt = [gates[:, 2 * DN_G + h:2 * DN_G + h + 1] for h in heads]
    decay = [jnp.exp(jnp.where(incl, gc[h] - grows[h:h + 1, :], NEG_INF)) for h in heads]
    pw = [-jnp.where(strict, kks[h // 2] * (beta[h] * decay[h]), 0.0) for h in heads]
    tinv = [eye + pw[h] for h in heads]
    sp = [_split_bf16(pw[h]) for h in heads]
    for _ in range(5):
        sp = [_split_bf16(_mm_presplit(sp[h], sp[h])) for h in heads]
        tinv = [tinv[h] + _mm_presplit(_split_bf16(tinv[h]), sp[h]) for h in heads]
    eg = [jnp.exp(gc[h]) for h in heads]
    uw = [_mm(tinv[h], jnp.concatenate([vs[h] * beta[h], ks[h // 2] * (beta[h] * eg[h])], axis=1))
          for h in heads]
    s_old = [s_scr[h] for h in heads]
    v_new = [uw[h][:, :DN_DK] - _mm(uw[h][:, DN_DK:], s_old[h]) for h in heads]
    outs = [_mm(qs[h // 2] * eg[h], s_old[h])
            + _mm(jnp.where(incl, qks[h // 2] * decay[h], 0.0), v_new[h]) for h in heads]
    o_ref[0] = jnp.concatenate(outs, axis=1)
    kd = [ks[h // 2] * jnp.exp(gt[h] - gc[h]) for h in heads]
    s_scr[...] = jnp.stack([s_old[h] * jnp.exp(gt[h][0:1, :]) + _mm(kd[h].T, v_new[h])
                            for h in heads], axis=0)


def dn_delta(qkv, gcol, grow, *, nb):
    t = qkv.shape[0]
    lt = t // nb
    nch = lt // DN_CHUNK
    nctx = CTX_LEN // DN_CHUNK
    ng = DN_HV // DN_G
    qw = DN_G // 2 * DN_DK
    vw = DN_G * DN_DK
    cidx = functools.partial(_dn_chunk_index, n_ctx_chunks=nctx, n_chunks=nch)
    return pl.pallas_call(
        _dn_delta_kernel,
        grid=(nb, ng, 2, nch),
        in_specs=[pl.BlockSpec((DN_CHUNK, qw), lambda b, g, d, c: (b * nch + cidx(d, c), g)),
                  pl.BlockSpec((DN_CHUNK, qw), lambda b, g, d, c: (b * nch + cidx(d, c),
                                                                   DN_QK // qw + g)),
                  pl.BlockSpec((DN_CHUNK, vw), lambda b, g, d, c: (b * nch + cidx(d, c),
                                                                   2 * DN_QK // vw + g)),
                  pl.BlockSpec((1, 1, 1, DN_CHUNK, 3 * DN_G),
                               lambda b, g, d, c: (b, d, g, cidx(d, c), 0)),
                  pl.BlockSpec((1, 1, 1, 1, DN_G, DN_CHUNK),
                               lambda b, g, d, c: (b, d, g, cidx(d, c), 0, 0))],
        out_specs=pl.BlockSpec((1, DN_CHUNK, vw), lambda b, g, d, c: (d, b * nch + cidx(d, c), g)),
        out_shape=jax.ShapeDtypeStruct((2, t, DN_VD), F32),
        scratch_shapes=[pltpu.VMEM((DN_G, DN_DK, DN_DK), F32)],
        compiler_params=_cparams(4),
        name="dn_delta",
    )(qkv, qkv, qkv, gcol, grow)


def _dn_gate_layouts(beta, gc, gt, *, nb):
    t = beta.shape[0]
    lt = t // nb
    nch = lt // DN_CHUNK
    ng = DN_HV // DN_G
    cols, rows = [], []
    for d in range(2):
        b0 = 2 * d * DN_HV
        parts = [beta[:, b0:b0 + DN_HV], gc[:, b0 + DN_HV:b0 + 2 * DN_HV],
                 gt[:, b0 + DN_HV:b0 + 2 * DN_HV]]
        parts = [p.reshape(nb, lt, ng, DN_G).transpose(0, 2, 1, 3) for p in parts]
        cols.append(jnp.concatenate(parts, axis=-1))
        rows.append(parts[1].reshape(nb, ng, nch, DN_CHUNK, DN_G).transpose(0, 1, 2, 4, 3))
    return jnp.stack(cols, axis=1), jnp.stack(rows, axis=1)


def _dn_gatenorm_kernel(o_ref, zs_ref, nw_ref, a_ref):
    for h in range(DN_HV):
        sl = slice(h * DN_DK, (h + 1) * DN_DK)
        o = o_ref[0, :, sl] + o_ref[1, :, sl]
        y = o * lax.rsqrt(jnp.mean(o * o, axis=-1, keepdims=True) + EPS) * nw_ref[...]
        a_ref[:, sl] = (y * zs_ref[:, sl].astype(F32)).astype(a_ref.dtype)


def dn_gatenorm(o2, zs, norm_w):
    _, t, k = o2.shape
    return pl.pallas_call(
        _dn_gatenorm_kernel,
        grid=(t // TM,),
        in_specs=[pl.BlockSpec((2, TM, k), lambda r: (0, r, 0)),
                  pl.BlockSpec((TM, k), lambda r: (r, 0)),
                  pl.BlockSpec((1, DN_DK), lambda r: (0, 0))],
        out_specs=pl.BlockSpec((TM, k), lambda r: (r, 0)),
        out_shape=jax.ShapeDtypeStruct((t, k), BF16),
        compiler_params=_cparams(1),
        name="dn_gatenorm",
    )(o2, zs, norm_w.reshape(1, DN_DK))


def _router_kernel(x_ref, nw_ref, sc_ref, sh_ref, wr_ref, br_ref, h_ref, eid_ref, gate_ref):
    h = _norm_mod(x_ref[...], nw_ref[...], sc_ref[0], sh_ref[0])
    h_ref[...] = h
    logits = jnp.dot(h, wr_ref[...], preferred_element_type=F32, precision=HIGHEST) + br_ref[...]
    lane = lax.broadcasted_iota(jnp.int32, logits.shape, 1)
    is_grp = lane < N_GROUPS
    mg = jnp.max(jnp.where(is_grp, logits, -jnp.inf), axis=-1, keepdims=True)
    eg = jnp.where(is_grp, jnp.exp(logits - mg), 0.0)
    pg = eg / jnp.sum(eg, axis=-1, keepdims=True)
    p_grp = jnp.max(pg, axis=-1, keepdims=True)
    g_sel = jnp.min(jnp.where(is_grp & (pg == p_grp), lane, LANES), axis=-1, keepdims=True)
    fine = lane - N_GROUPS
    in_grp = (fine >= 0) & (fine < N_EXPERTS) & ((fine // EXPERTS_PER_GROUP) == g_sel)
    mf = jnp.max(jnp.where(in_grp, logits, -jnp.inf), axis=-1, keepdims=True)
    ef = jnp.where(in_grp, jnp.exp(logits - mf), 0.0)
    pf = ef / jnp.sum(ef, axis=-1, keepdims=True)
    p1 = jnp.max(jnp.where(in_grp, pf, -1.0), axis=-1, keepdims=True)
    i1 = jnp.min(jnp.where(in_grp & (pf == p1), lane, LANES), axis=-1, keepdims=True)
    rest = in_grp & (lane != i1)
    p2 = jnp.max(jnp.where(rest, pf, -1.0), axis=-1, keepdims=True)
    i2 = jnp.min(jnp.where(rest & (pf == p2), lane, LANES), axis=-1, keepdims=True)
    den = p1 + p2
    eid_ref[...] = jnp.where(lane == 0, i1 - N_GROUPS, jnp.where(lane == 1, i2 - N_GROUPS, 0))
    gate_ref[...] = jnp.where(lane == 0, p_grp * p1 / den,
                              jnp.where(lane == 1, p_grp * p2 / den, 0.0))


def moe_router(s2d, nw, mod, wr, br, *, nb, sc_chunk, sh_chunk):
    t, d = s2d.shape
    tpb = t // nb // TM
    row = pl.BlockSpec((TM, d), lambda r: (r, 0))
    small = pl.BlockSpec((TM, LANES), lambda r: (r, 0))
    return pl.pallas_call(
        _router_kernel,
        grid=(t // TM,),
        in_specs=[row, pl.BlockSpec((1, d), lambda r: (0, 0)),
                  _mod_spec(nb, tpb, sc_chunk, d), _mod_spec(nb, tpb, sh_chunk, d),
                  pl.BlockSpec((d, LANES), lambda r: (0, 0)),
                  pl.BlockSpec((1, LANES), lambda r: (0, 0))],
        out_specs=[row, small, small],
        out_shape=[jax.ShapeDtypeStruct((t, d), F32),
                   jax.ShapeDtypeStruct((t, LANES), jnp.int32),
                   jax.ShapeDtypeStruct((t, LANES), F32)],
        compiler_params=_cparams(1),
        name="moe_router",
    )(s2d, nw.reshape(1, d), mod, mod, wr, br)


def _row_copy(src_hbm, row, dst, r, sem):
    return pltpu.make_async_copy(src_hbm.at[pl.ds(row, 1), :], dst.at[pl.ds(r, 1), :], sem)


def _moe_gather_kernel(tok_ref, nused_ref, h_hbm, o_ref, buf, sem):
    blk = pl.program_id(0)
    rows = buf.shape[0]

    @pl.when(blk < nused_ref[0])
    def _():
        def issue(r, carry):
            _row_copy(h_hbm, tok_ref[blk * rows + r], buf, r, sem).start()
            return carry
        lax.fori_loop(0, rows, issue, 0, unroll=8)

        def wait(r, carry):
            _row_copy(h_hbm, 0, buf, r, sem).wait()
            return carry
        lax.fori_loop(0, rows, wait, 0, unroll=8)
        o_ref[...] = buf[...].astype(BF16)

    @pl.when(blk >= nused_ref[0])
    def _():
        o_ref[...] = jnp.zeros_like(o_ref)


def moe_gather(h, buf_tok, n_used):
    t, d = h.shape
    p = buf_tok.shape[0]
    return pl.pallas_call(
        _moe_gather_kernel,
        grid_spec=pltpu.PrefetchScalarGridSpec(
            num_scalar_prefetch=2,
            grid=(p // MOE_ROWS,),
            in_specs=[pl.BlockSpec(memory_space=pl.ANY)],
            out_specs=pl.BlockSpec((MOE_ROWS, d), lambda i, *_: (i, 0)),
            scratch_shapes=[pltpu.VMEM((MOE_ROWS, d), F32), pltpu.SemaphoreType.DMA(())]),
        out_shape=jax.ShapeDtypeStruct((p, d), BF16),
        compiler_params=_cparams(1),
        name="moe_gather",
    )(buf_tok, n_used, h)


def _moe_expert_kernel(be_ref, nused_ref, x_ref, w13_ref, w2_ref, o_ref):
    blk = pl.program_id(0)

    @pl.when(blk < nused_ref[0])
    def _():
        h13 = jnp.dot(x_ref[...], w13_ref[0], preferred_element_type=F32)
        gate = h13[:, :D_EXPERT]
        act = gate * _sigmoid(gate) * h13[:, D_EXPERT:]
        o_ref[...] = jnp.dot(act.astype(BF16), w2_ref[0], preferred_element_type=F32)

    @pl.when(blk >= nused_ref[0])
    def _():
        o_ref[...] = jnp.zeros_like(o_ref)


def moe_experts(xs, block_e, n_used, w13, w2):
    p, d = xs.shape
    return pl.pallas_call(
        _moe_expert_kernel,
        grid_spec=pltpu.PrefetchScalarGridSpec(
            num_scalar_prefetch=2,
            grid=(p // MOE_ROWS,),
            in_specs=[pl.BlockSpec((MOE_ROWS, d), lambda i, be, nu: (i, 0)),
                      pl.BlockSpec((1, d, 2 * D_EXPERT), lambda i, be, nu: (be[i], 0, 0)),
                      pl.BlockSpec((1, D_EXPERT, d), lambda i, be, nu: (be[i], 0, 0))],
            out_specs=pl.BlockSpec((MOE_ROWS, d), lambda i, be, nu: (i, 0))),
        out_shape=jax.ShapeDtypeStruct((p, d), F32),
        compiler_params=_cparams(1),
        name="moe_experts",
    )(block_e, n_used, xs, w13, w2)


def _moe_combine_kernel(d0_ref, d1_ref, y_hbm, s_ref, gate_ref, g_ref, o_ref, buf0, buf1, sem):
    i = pl.program_id(0)
    rows = buf0.shape[0]

    def issue(r, carry):
        _row_copy(y_hbm, d0_ref[i * rows + r], buf0, r, sem.at[0]).start()
        _row_copy(y_hbm, d1_ref[i * rows + r], buf1, r, sem.at[1]).start()
        return carry
    lax.fori_loop(0, rows, issue, 0, unroll=8)

    def wait(r, carry):
        _row_copy(y_hbm, 0, buf0, r, sem.at[0]).wait()
        _row_copy(y_hbm, 0, buf1, r, sem.at[1]).wait()
        return carry
    lax.fori_loop(0, rows, wait, 0, unroll=8)
    gates = gate_ref[...]
    y = gates[:, 0:1] * buf0[...] + gates[:, 1:2] * buf1[...]
    o_ref[...] = s_ref[...] + g_ref[0] * y


def moe_combine(yb, dest0, dest1, s2d, gates, mod, *, nb, gate_chunk):
    t, d = s2d.shape
    tpb = t // nb // TM
    return pl.pallas_call(
        _moe_combine_kernel,
        grid_spec=pltpu.PrefetchScalarGridSpec(
            num_scalar_prefetch=2,
            grid=(t // TM,),
            in_specs=[pl.BlockSpec(memory_space=pl.ANY),
                      pl.BlockSpec((TM, d), lambda r, *_: (r, 0)),
                      pl.BlockSpec((TM, LANES), lambda r, *_: (r, 0)),
                      _mod_spec(nb, tpb, gate_chunk, d)],
            out_specs=pl.BlockSpec((TM, d), lambda r, *_: (r, 0)),
            scratch_shapes=[pltpu.VMEM((TM, d), F32), pltpu.VMEM((TM, d), F32),
                            pltpu.SemaphoreType.DMA((2,))]),
        out_shape=jax.ShapeDtypeStruct((t, d), F32),
        compiler_params=_cparams(1),
        name="moe_combine",
    )(dest0, dest1, yb, s2d, gates, mod)


def _moe_plan(eid):
    t = eid.shape[0]
    tk = t * TOP_K
    e_flat = eid[:, :TOP_K].reshape(tk)
    onehot = (e_flat[:, None] == jnp.arange(N_EXPERTS, dtype=jnp.int32)[None, :]).astype(jnp.int32)
    csum = jnp.cumsum(onehot, axis=0)
    rank = jnp.take_along_axis(csum, e_flat[:, None], axis=1)[:, 0] - 1
    counts = csum[-1]
    padded = (counts + MOE_ROWS - 1) // MOE_ROWS * MOE_ROWS
    pend = jnp.cumsum(padded)
    dest = (pend - padded)[e_flat] + rank
    n_blocks = tk // MOE_ROWS + N_EXPERTS
    buf_tok = jnp.zeros((n_blocks * MOE_ROWS,), jnp.int32).at[dest].set(
        jnp.arange(tk, dtype=jnp.int32) // TOP_K)
    block_e = jnp.searchsorted(pend, jnp.arange(n_blocks, dtype=jnp.int32) * MOE_ROWS, side='right')
    block_e = jnp.minimum(block_e, N_EXPERTS - 1).astype(jnp.int32)
    n_used = (pend[-1:] // MOE_ROWS).astype(jnp.int32)
    dest = dest.reshape(t, TOP_K)
    return buf_tok, block_e, n_used, dest[:, 0], dest[:, 1]


def hier_moe(s2d, nw, mod, wg, bg, we, be, w13, w2, *, nb):
    d = s2d.shape[1]
    pad = LANES - N_GROUPS - N_EXPERTS
    wr = jnp.concatenate([wg, we, jnp.zeros((d, pad), F32)], axis=1)
    br = jnp.concatenate([bg, be, jnp.zeros((pad,), F32)]).reshape(1, LANES)
    h, eid, gates = moe_router(s2d, nw, mod, wr, br, nb=nb, sc_chunk=4, sh_chunk=3)
    buf_tok, block_e, n_used, dest0, dest1 = _moe_plan(eid)
    xs = moe_gather(h, buf_tok, n_used)
    yb = moe_experts(xs, block_e, n_used, w13.astype(BF16), w2.astype(BF16))
    return moe_combine(yb, dest0, dest1, s2d, gates, mod, nb=nb, gate_chunk=5)


def _final_norm_kernel(x_ref, w_ref, o_ref):
    x = x_ref[...]
    o_ref[0] = x * lax.rsqrt(jnp.mean(x * x, axis=-1, keepdims=True) + EPS) * w_ref[...]


def final_norm(s2d, w, *, nb):
    t, d = s2d.shape
    lt = t // nb
    tpb = lt // TM
    ctx_tiles = CTX_LEN // TM
    return pl.pallas_call(
        _final_norm_kernel,
        grid=(nb, tpb - ctx_tiles),
        in_specs=[pl.BlockSpec((TM, d), lambda b, i: (b * tpb + ctx_tiles + i, 0)),
                  pl.BlockSpec((1, d), lambda b, i: (0, 0))],
        out_specs=pl.BlockSpec((1, TM, d), lambda b, i: (b, i, 0)),
        out_shape=jax.ShapeDtypeStruct((nb, lt - CTX_LEN, d), F32),
        compiler_params=_cparams(2),
        name="final_norm",
    )(s2d, w.reshape(1, d))


def _rope_tables(seq):
    dim = DA_DIM
    half = dim // 2
    rows = seq // GRID_W
    row, col = jnp.meshgrid(jnp.arange(rows), jnp.arange(GRID_W), indexing='ij')
    row = row.reshape(-1).astype(F32)
    col = col.reshape(-1).astype(F32)
    inv_freq = 1.0 / (ROPE_BASE ** (jnp.arange(0, half, 2, dtype=F32) / half))

    def table(pos):
        ang = pos[:, None] * inv_freq[None, :]
        ang = jnp.concatenate([ang, ang], axis=-1)
        return jnp.cos(ang), jnp.sin(ang)
    cr, sr = table(row)
    cc, scol = table(col)
    cos = jnp.concatenate([cr, cc], -1)
    sin = jnp.concatenate([sr, scol], -1)
    quarter = (jnp.arange(dim) // (half // 2)) % 2
    sa = jnp.where(quarter[None, :] == 0, -sin, 0.0)
    sb = jnp.where(quarter[None, :] == 1, sin, 0.0)
    ctx1 = jnp.ones((CTX_LEN, dim), F32)
    ctx0 = jnp.zeros((CTX_LEN, dim), F32)
    return tuple(jnp.tile(jnp.concatenate([c, x], axis=0), (1, LANES // dim))
                 for c, x in ((ctx1, cos), (ctx0, sa), (ctx0, sb)))


def token_mixer(i, s2d, mod, rope, p, *, nb):
    d = s2d.shape[1]
    kind = i % N_MIXERS
    j = i // N_MIXERS
    nw = p['norm_mix_w'][i]
    if kind == 0:
        lambda_init = 0.8 - 0.6 * math.exp(-0.3 * i)
        qkv = proj(s2d, nw, mod, p['da_w_qkv'][j].astype(BF16), nb=nb, tn=512, out_dtype=BF16,
                   epilogue="rope", rope=rope, rope_tiles=2 * d // 512)
        a = diff_attention(qkv, p['da_lambda'][j], p['da_subln_w'][j], nb=nb,
                           lambda_init=lambda_init)
        return out_proj(a, p['da_w_o'][j].astype(BF16), mod, s2d, nb=nb, gate_chunk=2)
    if kind == 1:
        w_in = p['dn_w_in'][j]
        qkv_pre = proj(s2d, nw, mod, w_in[:, :DN_QKV].astype(BF16), nb=nb, tn=512, out_dtype=F32)
        zs = proj(s2d, nw, mod, w_in[:, DN_QKV:DN_QKV + DN_VD].astype(BF16), nb=nb, tn=512,
                  out_dtype=BF16, epilogue="silu")
        ba = proj(s2d, nw, mod, w_in[:, DN_QKV + DN_VD:], nb=nb, tn=LANES, out_dtype=F32)
        qkv = dn_conv(qkv_pre, p['dn_conv_w'][j], nb=nb)
        beta, gc, gt = dn_gates(ba, p['dn_a_log'][j], p['dn_dt_bias'][j], nb=nb)
        gcol, grow = _dn_gate_layouts(beta, gc, gt, nb=nb)
        o2 = dn_delta(qkv, gcol, grow, nb=nb)
        a = dn_gatenorm(o2, zs, p['dn_norm_w'][j])
        return out_proj(a, p['dn_w_o'][j].astype(BF16), mod, s2d, nb=nb, gate_chunk=2)
    qkv = proj(s2d, nw, mod, p['wa_w_qkv'][j].astype(BF16), nb=nb, tn=256, out_dtype=BF16,
               epilogue="rope", rope=rope, rope_tiles=(WA_HQ + WA_HKV) * WA_DIM // 256)
    a = window_attention(qkv, p['wa_sinks'][j], nb=nb)
    return out_proj(a, p['wa_w_o'][j].astype(BF16), mod, s2d, nb=nb, gate_chunk=2)


def moe_sublayer(i, s2d, mod, p, *, nb):
    return hier_moe(s2d, p['norm_ffn_w'][i], mod, p['moe_wg'][i], p['moe_bg'][i], p['moe_we'][i],
                    p['moe_be'][i], p['moe_w13'][i], p['moe_w2'][i], nb=nb)


def kernel(x, c, ctx, c_ctx, ada_w, ada_b, norm_mix_w, norm_ffn_w, da_w_qkv, da_lambda, da_subln_w, da_w_o, dn_w_in, dn_conv_w, dn_a_log, dn_dt_bias, dn_norm_w, dn_w_o, wa_w_qkv, wa_sinks, wa_w_o, moe_wg, moe_bg, moe_we, moe_be, moe_w13, moe_w2, final_norm_w):
    p = dict(norm_mix_w=norm_mix_w, norm_ffn_w=norm_ffn_w, da_w_qkv=da_w_qkv,
             da_lambda=da_lambda, da_subln_w=da_subln_w, da_w_o=da_w_o, dn_w_in=dn_w_in,
             dn_conv_w=dn_conv_w, dn_a_log=dn_a_log, dn_dt_bias=dn_dt_bias, dn_norm_w=dn_norm_w,
             dn_w_o=dn_w_o, wa_w_qkv=wa_w_qkv, wa_sinks=wa_sinks, wa_w_o=wa_w_o, moe_wg=moe_wg,
             moe_bg=moe_bg, moe_we=moe_we, moe_be=moe_be, moe_w13=moe_w13, moe_w2=moe_w2)
    nb, seq, d = x.shape
    t = nb * (CTX_LEN + seq)
    s2d = jnp.concatenate([ctx, x], axis=1).reshape(t, d)
    cc = jnp.concatenate([c, c_ctx[None, :], jnp.zeros((16 - nb - 1, d), F32)], axis=0)
    mods = ada_tables(cc, ada_w, ada_b).reshape(DEPTH, 16, 1, 6 * d)
    rope = _rope_tables(seq)
    for i in range(DEPTH):
        s2d = token_mixer(i, s2d, mods[i], rope, p, nb=nb)
        s2d = moe_sublayer(i, s2d, mods[i], p, nb=nb)
    return final_norm(s2d, final_norm_w, nb=nb)
```

```python
import functools
import math

import jax
import jax.numpy as jnp
from jax import lax
from jax.experimental import pallas as pl
from jax.experimental.pallas import tpu as pltpu

F32 = jnp.float32
BF16 = jnp.bfloat16
HIGHEST = lax.Precision.HIGHEST

D_MODEL = 2048
DEPTH = 4
GRID_W = 64
CTX_LEN = 256
N_MIXERS = 3
EPS = 1e-6
ROPE_BASE = 10000.0
NEG_INF = -1e30
DA_HEADS = 16
DA_DIM = 64
DN_HK = 16
DN_HV = 32
DN_DK = 128
DN_CONV = 5
DN_CHUNK = 64
DN_QK = DN_HK * DN_DK
DN_VD = DN_HV * DN_DK
DN_QKV = 2 * DN_QK + DN_VD
WA_HQ = 32
WA_HKV = 4
WA_GROUP = 8
WA_DIM = 64
WINDOW = 128
N_GROUPS = 4
EXPERTS_PER_GROUP = 8
N_EXPERTS = 32
TOP_K = 2
D_EXPERT = 768

LANES = 128
TM = 256
BIG_TILES_PER_BATCH = 3
MOE_ROWS = 256
DN_G = 8
DA_TQ = 512
VMEM_LIMIT = 56 * 1024 * 1024


def _cparams(n_axes):
    return pltpu.CompilerParams(dimension_semantics=("arbitrary",) * n_axes,
                                vmem_limit_bytes=VMEM_LIMIT)


def _mm(a, b):
    return jnp.dot(a.astype(BF16), b.astype(BF16), preferred_element_type=F32)


def _mm_nt(a, b):
    return lax.dot_general(a.astype(BF16), b.astype(BF16), (((1,), (1,)), ((), ())),
                           preferred_element_type=F32)


def _split_bf16(a):
    hi = a.astype(BF16)
    return hi, (a - hi.astype(F32)).astype(BF16)


def _mm_presplit(a_parts, b_parts):
    b_hi, b_lo = b_parts
    m = a_parts[0].shape[0]
    o_hi = jnp.dot(jnp.concatenate(a_parts, axis=0), b_hi, preferred_element_type=F32)
    o_lo = jnp.dot(a_parts[0], b_lo, preferred_element_type=F32)
    return (o_hi[:m] + o_hi[m:]) + o_lo


def _sigmoid(x):
    return 1.0 / (1.0 + jnp.exp(-x))


def _mod_spec(nb, tpb, chunk, width):
    def imap(r, *_):
        return (jnp.where(r % tpb == 0, nb, r // tpb), 0, chunk)
    return pl.BlockSpec((1, 1, width), imap)


def _big_mod_specs(nb, tpb, chunk, width, col=None):
    def blk(j):
        return chunk if col is None else chunk * col + j
    return [pl.BlockSpec((1, 1, width), lambda r, j: (nb, 0, blk(j))),
            pl.BlockSpec((1, 1, width), lambda r, j: (r // tpb, 0, blk(j)))]


def _ctx_rows(rows, tiles_per_batch):
    first = (pl.program_id(0) % tiles_per_batch) * rows
    return first + lax.broadcasted_iota(jnp.int32, (rows, 1), 0) < CTX_LEN


def _ada_kernel(c_ref, w_ref, b_ref, o_ref):
    c = c_ref[...]
    o_ref[0] = jnp.dot(c * _sigmoid(c), w_ref[0], preferred_element_type=F32,
                       precision=HIGHEST) + b_ref[0]


def ada_tables(cc, ada_w, ada_b):
    depth, d, n = ada_w.shape
    tn = 768
    return pl.pallas_call(
        _ada_kernel,
        grid=(depth, n // tn),
        in_specs=[pl.BlockSpec((16, d), lambda l, j: (0, 0)),
                  pl.BlockSpec((1, d, tn), lambda l, j: (l, 0, j)),
                  pl.BlockSpec((1, 1, tn), lambda l, j: (l, 0, j))],
        out_specs=pl.BlockSpec((1, 16, tn), lambda l, j: (l, 0, j)),
        out_shape=jax.ShapeDtypeStruct((depth, 16, n), F32),
        compiler_params=_cparams(2),
        name="ada_tables",
    )(cc, ada_w, ada_b.reshape(depth, 1, n))


def _norm_mod(x, nw, sc, sh):
    y = x * lax.rsqrt(jnp.mean(x * x, axis=-1, keepdims=True) + EPS) * nw
    return y * (1.0 + sc) + sh


def _proj_kernel(*refs, epilogue, rope_tiles, tpb):
    if epilogue == "rope":
        (x_ref, nw_ref, scc_ref, scx_ref, shc_ref, shx_ref, w_ref, cos_ref, sa_ref, sb_ref,
         o_ref, h_scr) = refs
    else:
        x_ref, nw_ref, scc_ref, scx_ref, shc_ref, shx_ref, w_ref, o_ref, h_scr = refs
    j = pl.program_id(1)

    @pl.when(j == 0)
    def _():
        is_ctx = _ctx_rows(x_ref.shape[0], tpb)
        sc = jnp.where(is_ctx, scc_ref[0], scx_ref[0])
        sh = jnp.where(is_ctx, shc_ref[0], shx_ref[0])
        h_scr[...] = _norm_mod(x_ref[...], nw_ref[...], sc, sh).astype(h_scr.dtype)

    precision = HIGHEST if w_ref.dtype == F32 else None
    acc = jnp.dot(h_scr[...], w_ref[...], preferred_element_type=F32, precision=precision)
    if epilogue == "rope":
        @pl.when(j < rope_tiles)
        def _():
            cos, sa, sb = cos_ref[...], sa_ref[...], sb_ref[...]
            for g in range(acc.shape[1] // LANES):
                a = acc[:, g * LANES:(g + 1) * LANES]
                rot = a * cos + pltpu.roll(a, LANES - 16, 1) * sa + pltpu.roll(a, 16, 1) * sb
                o_ref[:, g * LANES:(g + 1) * LANES] = rot.astype(o_ref.dtype)

        @pl.when(j >= rope_tiles)
        def _():
            o_ref[...] = acc.astype(o_ref.dtype)
    elif epilogue == "silu":
        o_ref[...] = (acc * _sigmoid(acc)).astype(o_ref.dtype)
    else:
        o_ref[...] = acc.astype(o_ref.dtype)


def proj(s2d, nw, mod, w, *, nb, tn, out_dtype, epilogue=None, rope=None, rope_tiles=0):
    t, d = s2d.shape
    n = w.shape[1]
    tpb = BIG_TILES_PER_BATCH
    tm = t // nb // tpb
    in_specs = ([pl.BlockSpec((tm, d), lambda r, j: (r, 0)),
                 pl.BlockSpec((1, d), lambda r, j: (0, 0))]
                + _big_mod_specs(nb, tpb, 1, d) + _big_mod_specs(nb, tpb, 0, d)
                + [pl.BlockSpec((d, tn), lambda r, j: (0, j))])
    args = [s2d, nw.reshape(1, d), mod, mod, mod, mod, w]
    if epilogue == "rope":
        tab = pl.BlockSpec((tm, LANES), lambda r, j: (r % tpb, 0))
        in_specs += [tab, tab, tab]
        args += list(rope)
    return pl.pallas_call(
        functools.partial(_proj_kernel, epilogue=epilogue, rope_tiles=rope_tiles, tpb=tpb),
        grid=(t // tm, n // tn),
        in_specs=in_specs,
        out_specs=pl.BlockSpec((tm, tn), lambda r, j: (r, j)),
        out_shape=jax.ShapeDtypeStruct((t, n), out_dtype),
        scratch_shapes=[pltpu.VMEM((tm, d), w.dtype)],
        compiler_params=_cparams(2),
        name="proj",
    )(*args)


def _out_kernel(a_ref, w_ref, gc_ref, gx_ref, s_ref, o_ref, *, tpb):
    acc = jnp.dot(a_ref[...], w_ref[...], preferred_element_type=F32)
    gate = jnp.where(_ctx_rows(a_ref.shape[0], tpb), gc_ref[0], gx_ref[0])
    o_ref[...] = s_ref[...] + gate * acc


def out_proj(a, w, mod, s2d, *, nb, gate_chunk, tn=512):
    t, k = a.shape
    d = w.shape[1]
    tpb = BIG_TILES_PER_BATCH
    tm = t // nb // tpb
    return pl.pallas_call(
        functools.partial(_out_kernel, tpb=tpb),
        grid=(t // tm, d // tn),
        in_specs=[pl.BlockSpec((tm, k), lambda r, j: (r, 0)),
                  pl.BlockSpec((k, tn), lambda r, j: (0, j))]
                 + _big_mod_specs(nb, tpb, gate_chunk, tn, col=d // tn)
                 + [pl.BlockSpec((tm, tn), lambda r, j: (r, j))],
        out_specs=pl.BlockSpec((tm, tn), lambda r, j: (r, j)),
        out_shape=jax.ShapeDtypeStruct((t, d), F32),
        compiler_params=_cparams(2),
        name="out_proj",
    )(a, w, mod, mod, s2d)


def _da_kernel(lam_ref, q_ref, k_ref, v_ref, sw_ref, o_ref, klo, khi, *, lambda_init, n_ctx, tq):
    lam = lam_ref[...]
    l01 = jnp.sum(lam[0:1] * lam[1:2], axis=-1, keepdims=True)
    l23 = jnp.sum(lam[2:3] * lam[3:4], axis=-1, keepdims=True)
    lam_full = jnp.exp(l01) - jnp.exp(l23) + lambda_init
    lane = lax.broadcasted_iota(jnp.int32, (1, LANES), 1)
    k = k_ref[...]
    zero = jnp.zeros_like(k)
    klo[...] = jnp.where(lane < DA_DIM, k, zero)
    khi[...] = jnp.where(lane >= DA_DIM, k, zero)

    def attend(row0, rows, nk):
        q = q_ref[pl.ds(row0, rows), :] * (DA_DIM ** -0.5)
        v = v_ref[0:nk, :]
        s1 = _mm_nt(q, klo[0:nk, :])
        s2 = _mm_nt(q, khi[0:nk, :])
        e1 = jnp.exp(s1 - jnp.max(s1, axis=-1, keepdims=True))
        e2 = jnp.exp(s2 - jnp.max(s2, axis=-1, keepdims=True))
        o1 = _mm(e1, v) / jnp.sum(e1, axis=-1, keepdims=True)
        o2 = _mm(e2, v) / jnp.sum(e2, axis=-1, keepdims=True)
        o = o1 - lam_full * o2
        o = o * lax.rsqrt(jnp.mean(o * o, axis=-1, keepdims=True) + EPS) * sw_ref[...]
        o_ref[pl.ds(row0, rows), :] = (o * (1.0 - lambda_init)).astype(o_ref.dtype)

    attend(0, n_ctx, n_ctx)

    def latent_tile(i, carry):
        attend(pl.multiple_of(n_ctx + i * tq, LANES), tq, k_ref.shape[0])
        return carry
    lax.fori_loop(0, (q_ref.shape[0] - n_ctx) // tq, latent_tile, 0)


def diff_attention(qkv, lam, subln_w, *, nb, lambda_init):
    t = qkv.shape[0]
    lt = t // nb
    col = pl.BlockSpec((lt, LANES), lambda b, h: (b, h))
    return pl.pallas_call(
        functools.partial(_da_kernel, lambda_init=lambda_init, n_ctx=CTX_LEN, tq=DA_TQ),
        grid=(nb, DA_HEADS),
        in_specs=[pl.BlockSpec((4, DA_DIM), lambda b, h: (0, 0)),
                  col,
                  pl.BlockSpec((lt, LANES), lambda b, h: (b, DA_HEADS + h)),
                  pl.BlockSpec((lt, LANES), lambda b, h: (b, 2 * DA_HEADS + h)),
                  pl.BlockSpec((1, LANES), lambda b, h: (0, 0))],
        out_specs=col,
        out_shape=jax.ShapeDtypeStruct((t, D_MODEL), BF16),
        scratch_shapes=[pltpu.VMEM((lt, LANES), BF16)] * 2,
        compiler_params=_cparams(2),
        name="diff_attention",
    )(lam, qkv, qkv, qkv, subln_w.reshape(1, LANES))


def _wa_kernel(sink_ref, q_ref, k_ref, v_ref, o_ref, klo, khi, vlo, vhi, *, n_ctx, seq):
    hkv = pl.program_id(1)
    i = pl.program_id(2)
    tq = q_ref.shape[0]
    band = tq + 2 * WINDOW
    npair = WA_GROUP // 2
    lane = lax.broadcasted_iota(jnp.int32, (1, LANES), 1)
    n_ctx_tiles = n_ctx // tq

    @pl.when(i == 0)
    def _():
        odd = (hkv % 2) == 1
        for src, lo, hi in ((k_ref, klo, khi), (v_ref, vlo, vhi)):
            x = src[...].astype(F32)
            x = jnp.where(odd, pltpu.roll(x, WA_DIM, 1), x)
            x = jnp.where(lane < WA_DIM, x, 0.0)
            lo[...] = x.astype(BF16)
            hi[...] = pltpu.roll(x, WA_DIM, 1).astype(BF16)

    def attend(with_band):
        halves = 2
        hp = npair // halves
        rows = hp * tq
        qs = [jnp.concatenate([q_ref[:, p * LANES:(p + 1) * LANES]
                               for p in range(c * hp, (c + 1) * hp)], axis=0) * (WA_DIM ** -0.5)
              for c in range(halves)]
        chains = [(c, par) for c in range(halves) for par in range(2)]
        kv = ((klo, vlo), (khi, vhi))
        sink = [jnp.concatenate([jnp.full((tq, 1), sink_ref[hkv, 2 * p + par], F32)
                                 for p in range(c * hp, (c + 1) * hp)], axis=0)
                for c, par in chains]
        s_c = [_mm_nt(qs[c], kv[par][0][0:n_ctx, :]) for c, par in chains]
        m = [jnp.maximum(jnp.max(s_c[n], axis=-1, keepdims=True), sink[n])
             for n in range(len(chains))]
        if with_band:
            lat0 = (i - n_ctx_tiles) * tq
            start = jnp.clip(lat0 - WINDOW, 0, seq - band)
            row0 = pl.multiple_of(n_ctx + start, LANES)
            qpos = lat0 + (lax.broadcasted_iota(jnp.int32, (rows, band), 0) & (tq - 1))
            kpos = start + lax.broadcasted_iota(jnp.int32, (rows, band), 1)
            valid = jnp.abs(qpos - kpos) <= WINDOW
            s_b = [jnp.where(valid, _mm_nt(qs[c], kv[par][0][pl.ds(row0, band), :]), NEG_INF)
                   for c, par in chains]
            m = [jnp.maximum(m[n], jnp.max(s_b[n], axis=-1, keepdims=True))
                 for n in range(len(chains))]
            e_b = [jnp.exp(s_b[n] - m[n]) for n in range(len(chains))]
        e_c = [jnp.exp(s_c[n] - m[n]) for n in range(len(chains))]
        den = [jnp.sum(e_c[n], axis=-1, keepdims=True) + jnp.exp(sink[n] - m[n])
               for n in range(len(chains))]
        acc = [_mm(e_c[n], kv[par][1][0:n_ctx, :]) for n, (c, par) in enumerate(chains)]
        if with_band:
            den = [den[n] + jnp.sum(e_b[n], axis=-1, keepdims=True) for n in range(len(chains))]
            acc = [acc[n] + _mm(e_b[n], kv[par][1][pl.ds(row0, band), :])
                   for n, (c, par) in enumerate(chains)]
        out = [acc[2 * c] / den[2 * c] + acc[2 * c + 1] / den[2 * c + 1] for c in range(halves)]
        for p in range(npair):
            c, r = divmod(p, hp)
            o_ref[:, p * LANES:(p + 1) * LANES] = out[c][r * tq:(r + 1) * tq].astype(o_ref.dtype)

    @pl.when(i < n_ctx_tiles)
    def _():
        attend(False)

    @pl.when(i >= n_ctx_tiles)
    def _():
        attend(True)


def window_attention(qkv, sinks, *, nb):
    t = qkv.shape[0]
    lt = t // nb
    tq = 128
    nq = lt // tq
    qw = WA_GROUP * WA_DIM
    kblk = WA_HQ * WA_DIM // LANES
    vblk = kblk + WA_HKV * WA_DIM // LANES
    return pl.pallas_call(
        functools.partial(_wa_kernel, n_ctx=CTX_LEN, seq=lt - CTX_LEN),
        grid=(nb, WA_HKV, nq),
        in_specs=[pl.BlockSpec(memory_space=pltpu.SMEM),
                  pl.BlockSpec((tq, qw), lambda b, h, i: (b * nq + i, h)),
                  pl.BlockSpec((lt, LANES), lambda b, h, i: (b, kblk + h // 2)),
                  pl.BlockSpec((lt, LANES), lambda b, h, i: (b, vblk + h // 2))],
        out_specs=pl.BlockSpec((tq, qw), lambda b, h, i: (b * nq + i, h)),
        out_shape=jax.ShapeDtypeStruct((t, D_MODEL), BF16),
        scratch_shapes=[pltpu.VMEM((lt, LANES), BF16)] * 4,
        compiler_params=_cparams(3),
        name="window_attention",
    )(sinks.reshape(WA_HKV, WA_GROUP), qkv, qkv, qkv)


def _dn_conv_kernel(x_ref, w_ref, o_ref, *, n_ctx):
    cb = pl.program_id(1)
    x = x_ref[...]
    n = x.shape[0]
    w = w_ref[...]
    t = lax.broadcasted_iota(jnp.int32, (n, 1), 0)
    lo = jnp.where(t < n_ctx, 0, n_ctx)
    hi = jnp.where(t < n_ctx, n_ctx, n)
    acc = jnp.zeros_like(x)
    for j in range(DN_CONV):
        sft = j - (DN_CONV - 1) // 2
        xs = x if sft == 0 else pltpu.roll(x, (-sft) % n, 0)
        u = t + sft
        ok = (u >= lo) & (u < hi)
        acc = acc + jnp.where(ok, xs, 0.0) * w[j:j + 1, :]
    y = acc * _sigmoid(acc)
    nrm = lax.rsqrt(jnp.sum(y * y, axis=-1, keepdims=True) + EPS)
    scl = jnp.where(cb < DN_HK, nrm * (DN_DK ** -0.5), jnp.where(cb < 2 * DN_HK, nrm, 1.0))
    o_ref[...] = y * scl


def dn_conv(qkv_pre, conv_w, *, nb):
    t, n = qkv_pre.shape
    lt = t // nb
    return pl.pallas_call(
        functools.partial(_dn_conv_kernel, n_ctx=CTX_LEN),
        grid=(nb, n // LANES),
        in_specs=[pl.BlockSpec((lt, LANES), lambda b, c: (b, c)),
                  pl.BlockSpec((DN_CONV, LANES), lambda b, c: (0, c))],
        out_specs=pl.BlockSpec((lt, LANES), lambda b, c: (b, c)),
        out_shape=jax.ShapeDtypeStruct((t, n), F32),
        compiler_params=_cparams(2),
        name="dn_conv",
    )(qkv_pre, conv_w)


def _dn_gate_kernel(ba_ref, alog_ref, dtb_ref, beta_ref, gc_ref, gt_ref):
    n = ba_ref.shape[0]
    c = DN_CHUNK
    ba = ba_ref[...]
    beta_ref[...] = _sigmoid(ba)
    z = ba + dtb_ref[...]
    softplus = jnp.maximum(z, 0.0) + jnp.log(1.0 + jnp.exp(-jnp.abs(z)))
    g = -jnp.exp(alog_ref[...]) * softplus
    row = lax.broadcasted_iota(jnp.int32, (c, c), 0)
    col = lax.broadcasted_iota(jnp.int32, (c, c), 1)
    prefix = (col <= row).astype(F32)
    suffix = (col >= row).astype(F32)
    ones = jnp.ones((c, c), F32)
    fwd_lane = lax.broadcasted_iota(jnp.int32, (1, LANES), 1) < 2 * DN_HV
    for ci in range(n // c):
        gch = g[ci * c:(ci + 1) * c, :]
        csum = jnp.where(fwd_lane,
                         jnp.dot(prefix, gch, preferred_element_type=F32, precision=HIGHEST),
                         jnp.dot(suffix, gch, preferred_element_type=F32, precision=HIGHEST))
        gc_ref[ci * c:(ci + 1) * c, :] = csum
        gt_ref[ci * c:(ci + 1) * c, :] = jnp.dot(ones, gch, preferred_element_type=F32,
                                                 precision=HIGHEST)


def dn_gates(ba, a_log, dt_bias, *, nb):
    t = ba.shape[0]
    lt = t // nb
    zeros = jnp.zeros((DN_HV,), F32)
    alog = jnp.concatenate([zeros, a_log[0], zeros, a_log[1]]).reshape(1, LANES)
    dtb = jnp.concatenate([zeros, dt_bias[0], zeros, dt_bias[1]]).reshape(1, LANES)
    blk = pl.BlockSpec((lt, LANES), lambda b: (b, 0))
    vec = pl.BlockSpec((1, LANES), lambda b: (0, 0))
    shp = jax.ShapeDtypeStruct((t, LANES), F32)
    return pl.pallas_call(
        _dn_gate_kernel,
        grid=(nb,),
        in_specs=[blk, vec, vec],
        out_specs=[blk, blk, blk],
        out_shape=[shp, shp, shp],
        compiler_params=_cparams(1),
        name="dn_gates",
    )(ba, alog, dtb)


def _dn_chunk_index(d, c, n_ctx_chunks, n_chunks):
    bwd = jnp.where(c < n_ctx_chunks, n_ctx_chunks - 1 - c, n_chunks - 1 + n_ctx_chunks - c)
    return jnp.where(d == 0, c, bwd)


def _dn_delta_kernel(q_ref, k_ref, v_ref, gcol_ref, grow_ref, o_ref, s_scr):
    d = pl.program_id(2)
    c = pl.program_id(3)
    cs = DN_CHUNK

    @pl.when(c == 0)
    def _():
        s_scr[...] = jnp.zeros_like(s_scr)

    row = lax.broadcasted_iota(jnp.int32, (cs, cs), 0)
    col = lax.broadcasted_iota(jnp.int32, (cs, cs), 1)
    ahead = (row - col) * jnp.where(d == 0, 1, -1)
    incl = ahead >= 0
    strict = ahead > 0
    eye = (row == col).astype(F32)

    gates = gcol_ref[0, 0, 0]
    grows = grow_ref[0, 0, 0, 0]
    heads = range(DN_G)
    qs = [q_ref[:, hk * DN_DK:(hk + 1) * DN_DK] for hk in range(DN_G // 2)]
    ks = [k_ref[:, hk * DN_DK:(hk + 1) * DN_DK] for hk in range(DN_G // 2)]
    vs = [v_ref[:, h * DN_DK:(h + 1) * DN_DK] for h in heads]
    kks = [_mm_nt(k, k) for k in ks]
    qks = [_mm_nt(q, k) for q, k in zip(qs, ks)]
    beta = [gates[:, h:h + 1] for h in heads]
    gc = [gates[:, DN_G + h:DN_G + h + 1] for h in heads]
    gt = [gates[:, 2 * DN_G + h:2 * DN_G + h + 1] for h in heads]
    decay = [jnp.exp(jnp.where(incl, gc[h] - grows[h:h + 1, :], NEG_INF)) for h in heads]
    pw = [-jnp.where(strict, kks[h // 2] * (beta[h] * decay[h]), 0.0) for h in heads]
    tinv = [eye + pw[h] for h in heads]
    sp = [_split_bf16(pw[h]) for h in heads]
    for _ in range(5):
        sp = [_split_bf16(_mm_presplit(sp[h], sp[h])) for h in heads]
        tinv = [tinv[h] + _mm_presplit(_split_bf16(tinv[h]), sp[h]) for h in heads]
    eg = [jnp.exp(gc[h]) for h in heads]
    uw = [_mm(tinv[h], jnp.concatenate([vs[h] * beta[h], ks[h // 2] * (beta[h] * eg[h])], axis=1))
          for h in heads]
    s_old = [s_scr[h] for h in heads]
    v_new = [uw[h][:, :DN_DK] - _mm(uw[h][:, DN_DK:], s_old[h]) for h in heads]
    outs = [_mm(qs[h // 2] * eg[h], s_old[h])
            + _mm(jnp.where(incl, qks[h // 2] * decay[h], 0.0), v_new[h]) for h in heads]
    o_ref[0] = jnp.concatenate(outs, axis=1)
    kd = [ks[h // 2] * jnp.exp(gt[h] - gc[h]) for h in heads]
    s_scr[...] = jnp.stack([s_old[h] * jnp.exp(gt[h][0:1, :]) + _mm(kd[h].T, v_new[h])
                            for h in heads], axis=0)


def dn_delta(qkv, gcol, grow, *, nb):
    t = qkv.shape[0]
    lt = t // nb
    nch = lt // DN_CHUNK
    nctx = CTX_LEN // DN_CHUNK
    ng = DN_HV // DN_G
    qw = DN_G // 2 * DN_DK
    vw = DN_G * DN_DK
    cidx = functools.partial(_dn_chunk_index, n_ctx_chunks=nctx, n_chunks=nch)
    return pl.pallas_call(
        _dn_delta_kernel,
        grid=(nb, ng, 2, nch),
        in_specs=[pl.BlockSpec((DN_CHUNK, qw), lambda b, g, d, c: (b * nch + cidx(d, c), g)),
                  pl.BlockSpec((DN_CHUNK, qw), lambda b, g, d, c: (b * nch + cidx(d, c),
                                                                   DN_QK // qw + g)),
                  pl.BlockSpec((DN_CHUNK, vw), lambda b, g, d, c: (b * nch + cidx(d, c),
                                                                   2 * DN_QK // vw + g)),
                  pl.BlockSpec((1, 1, 1, DN_CHUNK, 3 * DN_G),
                               lambda b, g, d, c: (b, d, g, cidx(d, c), 0)),
                  pl.BlockSpec((1, 1, 1, 1, DN_G, DN_CHUNK),
                               lambda b, g, d, c: (b, d, g, cidx(d, c), 0, 0))],
        out_specs=pl.BlockSpec((1, DN_CHUNK, vw), lambda b, g, d, c: (d, b * nch + cidx(d, c), g)),
        out_shape=jax.ShapeDtypeStruct((2, t, DN_VD), F32),
        scratch_shapes=[pltpu.VMEM((DN_G, DN_DK, DN_DK), F32)],
        compiler_params=_cparams(4),
        name="dn_delta",
    )(qkv, qkv, qkv, gcol, grow)


def _dn_gate_layouts(beta, gc, gt, *, nb):
    t = beta.shape[0]
    lt = t // nb
    nch = lt // DN_CHUNK
    ng = DN_HV // DN_G
    cols, rows = [], []
    for d in range(2):
        b0 = 2 * d * DN_HV
        parts = [beta[:, b0:b0 + DN_HV], gc[:, b0 + DN_HV:b0 + 2 * DN_HV],
                 gt[:, b0 + DN_HV:b0 + 2 * DN_HV]]
        parts = [p.reshape(nb, lt, ng, DN_G).transpose(0, 2, 1, 3) for p in parts]
        cols.append(jnp.concatenate(parts, axis=-1))
        rows.append(parts[1].reshape(nb, ng, nch, DN_CHUNK, DN_G).transpose(0, 1, 2, 4, 3))
    return jnp.stack(cols, axis=1), jnp.stack(rows, axis=1)


def _dn_gatenorm_kernel(o_ref, zs_ref, nw_ref, a_ref):
    for h in range(DN_HV):
        sl = slice(h * DN_DK, (h + 1) * DN_DK)
        o = o_ref[0, :, sl] + o_ref[1, :, sl]
        y = o * lax.rsqrt(jnp.mean(o * o, axis=-1, keepdims=True) + EPS) * nw_ref[...]
        a_ref[:, sl] = (y * zs_ref[:, sl].astype(F32)).astype(a_ref.dtype)


def dn_gatenorm(o2, zs, norm_w):
    _, t, k = o2.shape
    return pl.pallas_call(
        _dn_gatenorm_kernel,
        grid=(t // TM,),
        in_specs=[pl.BlockSpec((2, TM, k), lambda r: (0, r, 0)),
                  pl.BlockSpec((TM, k), lambda r: (r, 0)),
                  pl.BlockSpec((1, DN_DK), lambda r: (0, 0))],
        out_specs=pl.BlockSpec((TM, k), lambda r: (r, 0)),
        out_shape=jax.ShapeDtypeStruct((t, k), BF16),
        compiler_params=_cparams(1),
        name="dn_gatenorm",
    )(o2, zs, norm_w.reshape(1, DN_DK))


def _router_kernel(x_ref, nw_ref, sc_ref, sh_ref, wr_ref, br_ref, h_ref, eid_ref, gate_ref):
    h = _norm_mod(x_ref[...], nw_ref[...], sc_ref[0], sh_ref[0])
    h_ref[...] = h
    logits = jnp.dot(h, wr_ref[...], preferred_element_type=F32, precision=HIGHEST) + br_ref[...]
    lane = lax.broadcasted_iota(jnp.int32, logits.shape, 1)
    is_grp = lane < N_GROUPS
    mg = jnp.max(jnp.where(is_grp, logits, -jnp.inf), axis=-1, keepdims=True)
    eg = jnp.where(is_grp, jnp.exp(logits - mg), 0.0)
    pg = eg / jnp.sum(eg, axis=-1, keepdims=True)
    p_grp = jnp.max(pg, axis=-1, keepdims=True)
    g_sel = jnp.min(jnp.where(is_grp & (pg == p_grp), lane, LANES), axis=-1, keepdims=True)
    fine = lane - N_GROUPS
    in_grp = (fine >= 0) & (fine < N_EXPERTS) & ((fine // EXPERTS_PER_GROUP) == g_sel)
    mf = jnp.max(jnp.where(in_grp, logits, -jnp.inf), axis=-1, keepdims=True)
    ef = jnp.where(in_grp, jnp.exp(logits - mf), 0.0)
    pf = ef / jnp.sum(ef, axis=-1, keepdims=True)
    p1 = jnp.max(jnp.where(in_grp, pf, -1.0), axis=-1, keepdims=True)
    i1 = jnp.min(jnp.where(in_grp & (pf == p1), lane, LANES), axis=-1, keepdims=True)
    rest = in_grp & (lane != i1)
    p2 = jnp.max(jnp.where(rest, pf, -1.0), axis=-1, keepdims=True)
    i2 = jnp.min(jnp.where(rest & (pf == p2), lane, LANES), axis=-1, keepdims=True)
    den = p1 + p2
    eid_ref[...] = jnp.where(lane == 0, i1 - N_GROUPS, jnp.where(lane == 1, i2 - N_GROUPS, 0))
    gate_ref[...] = jnp.where(lane == 0, p_grp * p1 / den,
                              jnp.where(lane == 1, p_grp * p2 / den, 0.0))


def moe_router(s2d, nw, mod, wr, br, *, nb, sc_chunk, sh_chunk):
    t, d = s2d.shape
    tpb = t // nb // TM
    row = pl.BlockSpec((TM, d), lambda r: (r, 0))
    small = pl.BlockSpec((TM, LANES), lambda r: (r, 0))
    return pl.pallas_call(
        _router_kernel,
        grid=(t // TM,),
        in_specs=[row, pl.BlockSpec((1, d), lambda r: (0, 0)),
                  _mod_spec(nb, tpb, sc_chunk, d), _mod_spec(nb, tpb, sh_chunk, d),
                  pl.BlockSpec((d, LANES), lambda r: (0, 0)),
                  pl.BlockSpec((1, LANES), lambda r: (0, 0))],
        out_specs=[row, small, small],
        out_shape=[jax.ShapeDtypeStruct((t, d), F32),
                   jax.ShapeDtypeStruct((t, LANES), jnp.int32),
                   jax.ShapeDtypeStruct((t, LANES), F32)],
        compiler_params=_cparams(1),
        name="moe_router",
    )(s2d, nw.reshape(1, d), mod, mod, wr, br)


def _row_copy(src_hbm, row, dst, r, sem):
    return pltpu.make_async_copy(src_hbm.at[pl.ds(row, 1), :], dst.at[pl.ds(r, 1), :], sem)


def _moe_gather_kernel(tok_ref, nused_ref, h_hbm, o_ref, buf, sem):
    blk = pl.program_id(0)
    rows = buf.shape[1]
    n_used = nused_ref[0]

    def fetch(b):
        slot = b % 2

        def issue(r, carry):
            _row_copy(h_hbm, tok_ref[b * rows + r], buf.at[slot], r, sem.at[slot]).start()
            return carry
        lax.fori_loop(0, rows, issue, 0, unroll=8)

    @pl.when(blk == 0)
    def _():
        fetch(blk)

    @pl.when(blk + 1 < n_used)
    def _():
        fetch(blk + 1)

    @pl.when(blk < n_used)
    def _():
        slot = blk % 2

        def wait(r, carry):
            _row_copy(h_hbm, 0, buf.at[slot], r, sem.at[slot]).wait()
            return carry
        lax.fori_loop(0, rows, wait, 0, unroll=8)
        o_ref[...] = buf[slot].astype(BF16)

    @pl.when(blk >= n_used)
    def _():
        o_ref[...] = jnp.zeros_like(o_ref)


def moe_gather(h, buf_tok, n_used):
    t, d = h.shape
    p = buf_tok.shape[0]
    return pl.pallas_call(
        _moe_gather_kernel,
        grid_spec=pltpu.PrefetchScalarGridSpec(
            num_scalar_prefetch=2,
            grid=(p // MOE_ROWS,),
            in_specs=[pl.BlockSpec(memory_space=pl.ANY)],
            out_specs=pl.BlockSpec((MOE_ROWS, d), lambda i, *_: (i, 0)),
            scratch_shapes=[pltpu.VMEM((2, MOE_ROWS, d), F32), pltpu.SemaphoreType.DMA((2,))]),
        out_shape=jax.ShapeDtypeStruct((p, d), BF16),
        compiler_params=_cparams(1),
        name="moe_gather",
    )(buf_tok, n_used, h)


def _moe_expert_kernel(be_ref, nused_ref, x_ref, w13_ref, w2_ref, o_ref):
    blk = pl.program_id(0)

    @pl.when(blk < nused_ref[0])
    def _():
        h13 = jnp.dot(x_ref[...], w13_ref[0], preferred_element_type=F32)
        gate = h13[:, :D_EXPERT]
        act = gate * _sigmoid(gate) * h13[:, D_EXPERT:]
        o_ref[...] = jnp.dot(act.astype(BF16), w2_ref[0], preferred_element_type=F32)

    @pl.when(blk >= nused_ref[0])
    def _():
        o_ref[...] = jnp.zeros_like(o_ref)


def moe_experts(xs, block_e, n_used, w13, w2):
    p, d = xs.shape
    return pl.pallas_call(
        _moe_expert_kernel,
        grid_spec=pltpu.PrefetchScalarGridSpec(
            num_scalar_prefetch=2,
            grid=(p // MOE_ROWS,),
            in_specs=[pl.BlockSpec((MOE_ROWS, d), lambda i, be, nu: (i, 0)),
                      pl.BlockSpec((1, d, 2 * D_EXPERT), lambda i, be, nu: (be[i], 0, 0)),
                      pl.BlockSpec((1, D_EXPERT, d), lambda i, be, nu: (be[i], 0, 0))],
            out_specs=pl.BlockSpec((MOE_ROWS, d), lambda i, be, nu: (i, 0))),
        out_shape=jax.ShapeDtypeStruct((p, d), F32),
        compiler_params=_cparams(1),
        name="moe_experts",
    )(block_e, n_used, xs, w13, w2)


def _moe_combine_kernel(d0_ref, d1_ref, y_hbm, s_ref, gate_ref, g_ref, o_ref, buf, sem):
    i = pl.program_id(0)
    rows = buf.shape[2]

    def fetch(tile):
        slot = tile % 2

        def issue(r, carry):
            _row_copy(y_hbm, d0_ref[tile * rows + r], buf.at[slot, 0], r, sem.at[slot, 0]).start()
            _row_copy(y_hbm, d1_ref[tile * rows + r], buf.at[slot, 1], r, sem.at[slot, 1]).start()
            return carry
        lax.fori_loop(0, rows, issue, 0, unroll=8)

    @pl.when(i == 0)
    def _():
        fetch(i)

    @pl.when(i + 1 < pl.num_programs(0))
    def _():
        fetch(i + 1)

    slot = i % 2

    def wait(r, carry):
        _row_copy(y_hbm, 0, buf.at[slot, 0], r, sem.at[slot, 0]).wait()
        _row_copy(y_hbm, 0, buf.at[slot, 1], r, sem.at[slot, 1]).wait()
        return carry
    lax.fori_loop(0, rows, wait, 0, unroll=8)
    gates = gate_ref[...]
    y = gates[:, 0:1] * buf[slot, 0] + gates[:, 1:2] * buf[slot, 1]
    o_ref[...] = s_ref[...] + g_ref[0] * y


def moe_combine(yb, dest0, dest1, s2d, gates, mod, *, nb, gate_chunk):
    t, d = s2d.shape
    tpb = t // nb // TM
    return pl.pallas_call(
        _moe_combine_kernel,
        grid_spec=pltpu.PrefetchScalarGridSpec(
            num_scalar_prefetch=2,
            grid=(t // TM,),
            in_specs=[pl.BlockSpec(memory_space=pl.ANY),
                      pl.BlockSpec((TM, d), lambda r, *_: (r, 0)),
                      pl.BlockSpec((TM, LANES), lambda r, *_: (r, 0)),
                      _mod_spec(nb, tpb, gate_chunk, d)],
            out_specs=pl.BlockSpec((TM, d), lambda r, *_: (r, 0)),
            scratch_shapes=[pltpu.VMEM((2, 2, TM, d), F32), pltpu.SemaphoreType.DMA((2, 2))]),
        out_shape=jax.ShapeDtypeStruct((t, d), F32),
        compiler_params=_cparams(1),
        name="moe_combine",
    )(dest0, dest1, yb, s2d, gates, mod)


def _moe_plan(eid):
    t = eid.shape[0]
    tk = t * TOP_K
    e_flat = eid[:, :TOP_K].reshape(tk)
    onehot = (e_flat[:, None] == jnp.arange(N_EXPERTS, dtype=jnp.int32)[None, :]).astype(jnp.int32)
    csum = jnp.cumsum(onehot, axis=0)
    rank = jnp.take_along_axis(csum, e_flat[:, None], axis=1)[:, 0] - 1
    counts = csum[-1]
    padded = (counts + MOE_ROWS - 1) // MOE_ROWS * MOE_ROWS
    pend = jnp.cumsum(padded)
    dest = (pend - padded)[e_flat] + rank
    n_blocks = tk // MOE_ROWS + N_EXPERTS
    buf_tok = jnp.zeros((n_blocks * MOE_ROWS,), jnp.int32).at[dest].set(
        jnp.arange(tk, dtype=jnp.int32) // TOP_K)
    block_e = jnp.searchsorted(pend, jnp.arange(n_blocks, dtype=jnp.int32) * MOE_ROWS, side='right')
    block_e = jnp.minimum(block_e, N_EXPERTS - 1).astype(jnp.int32)
    n_used = (pend[-1:] // MOE_ROWS).astype(jnp.int32)
    dest = dest.reshape(t, TOP_K)
    return buf_tok, block_e, n_used, dest[:, 0], dest[:, 1]


def hier_moe(s2d, nw, mod, wg, bg, we, be, w13, w2, *, nb):
    d = s2d.shape[1]
    pad = LANES - N_GROUPS - N_EXPERTS
    wr = jnp.concatenate([wg, we, jnp.zeros((d, pad), F32)], axis=1)
    br = jnp.concatenate([bg, be, jnp.zeros((pad,), F32)]).reshape(1, LANES)
    h, eid, gates = moe_router(s2d, nw, mod, wr, br, nb=nb, sc_chunk=4, sh_chunk=3)
    buf_tok, block_e, n_used, dest0, dest1 = _moe_plan(eid)
    xs = moe_gather(h, buf_tok, n_used)
    yb = moe_experts(xs, block_e, n_used, w13.astype(BF16), w2.astype(BF16))
    return moe_combine(yb, dest0, dest1, s2d, gates, mod, nb=nb, gate_chunk=5)


def _final_norm_kernel(x_ref, w_ref, o_ref):
    x = x_ref[...]
    o_ref[0] = x * lax.rsqrt(jnp.mean(x * x, axis=-1, keepdims=True) + EPS) * w_ref[...]


def final_norm(s2d, w, *, nb):
    t, d = s2d.shape
    lt = t // nb
    tpb = lt // TM
    ctx_tiles = CTX_LEN // TM
    return pl.pallas_call(
        _final_norm_kernel,
        grid=(nb, tpb - ctx_tiles),
        in_specs=[pl.BlockSpec((TM, d), lambda b, i: (b * tpb + ctx_tiles + i, 0)),
                  pl.BlockSpec((1, d), lambda b, i: (0, 0))],
        out_specs=pl.BlockSpec((1, TM, d), lambda b, i: (b, i, 0)),
        out_shape=jax.ShapeDtypeStruct((nb, lt - CTX_LEN, d), F32),
        compiler_params=_cparams(2),
        name="final_norm",
    )(s2d, w.reshape(1, d))


def _rope_tables(seq):
    dim = DA_DIM
    half = dim // 2
    rows = seq // GRID_W
    row, col = jnp.meshgrid(jnp.arange(rows), jnp.arange(GRID_W), indexing='ij')
    row = row.reshape(-1).astype(F32)
    col = col.reshape(-1).astype(F32)
    inv_freq = 1.0 / (ROPE_BASE ** (jnp.arange(0, half, 2, dtype=F32) / half))

    def table(pos):
        ang = pos[:, None] * inv_freq[None, :]
        ang = jnp.concatenate([ang, ang], axis=-1)
        return jnp.cos(ang), jnp.sin(ang)
    cr, sr = table(row)
    cc, scol = table(col)
    cos = jnp.concatenate([cr, cc], -1)
    sin = jnp.concatenate([sr, scol], -1)
    quarter = (jnp.arange(dim) // (half // 2)) % 2
    sa = jnp.where(quarter[None, :] == 0, -sin, 0.0)
    sb = jnp.where(quarter[None, :] == 1, sin, 0.0)
    ctx1 = jnp.ones((CTX_LEN, dim), F32)
    ctx0 = jnp.zeros((CTX_LEN, dim), F32)
    return tuple(jnp.tile(jnp.concatenate([c, x], axis=0), (1, LANES // dim))
                 for c, x in ((ctx1, cos), (ctx0, sa), (ctx0, sb)))


def token_mixer(i, s2d, mod, rope, p, *, nb):
    d = s2d.shape[1]
    kind = i % N_MIXERS
    j = i // N_MIXERS
    nw = p['norm_mix_w'][i]
    if kind == 0:
        lambda_init = 0.8 - 0.6 * math.exp(-0.3 * i)
        qkv = proj(s2d, nw, mod, p['da_w_qkv'][j].astype(BF16), nb=nb, tn=512, out_dtype=BF16,
                   epilogue="rope", rope=rope, rope_tiles=2 * d // 512)
        a = diff_attention(qkv, p['da_lambda'][j], p['da_subln_w'][j], nb=nb,
                           lambda_init=lambda_init)
        return out_proj(a, p['da_w_o'][j].astype(BF16), mod, s2d, nb=nb, gate_chunk=2)
    if kind == 1:
        w_in = p['dn_w_in'][j]
        qkv_pre = proj(s2d, nw, mod, w_in[:, :DN_QKV].astype(BF16), nb=nb, tn=512, out_dtype=F32)
        zs = proj(s2d, nw, mod, w_in[:, DN_QKV:DN_QKV + DN_VD].astype(BF16), nb=nb, tn=512,
                  out_dtype=BF16, epilogue="silu")
        ba = proj(s2d, nw, mod, w_in[:, DN_QKV + DN_VD:], nb=nb, tn=LANES, out_dtype=F32)
        qkv = dn_conv(qkv_pre, p['dn_conv_w'][j], nb=nb)
        beta, gc, gt = dn_gates(ba, p['dn_a_log'][j], p['dn_dt_bias'][j], nb=nb)
        gcol, grow = _dn_gate_layouts(beta, gc, gt, nb=nb)
        o2 = dn_delta(qkv, gcol, grow, nb=nb)
        a = dn_gatenorm(o2, zs, p['dn_norm_w'][j])
        return out_proj(a, p['dn_w_o'][j].astype(BF16), mod, s2d, nb=nb, gate_chunk=2)
    qkv = proj(s2d, nw, mod, p['wa_w_qkv'][j].astype(BF16), nb=nb, tn=256, out_dtype=BF16,
               epilogue="rope", rope=rope, rope_tiles=(WA_HQ + WA_HKV) * WA_DIM // 256)
    a = window_attention(qkv, p['wa_sinks'][j], nb=nb)
    return out_proj(a, p['wa_w_o'][j].astype(BF16), mod, s2d, nb=nb, gate_chunk=2)


def moe_sublayer(i, s2d, mod, p, *, nb):
    return hier_moe(s2d, p['norm_ffn_w'][i], mod, p['moe_wg'][i], p['moe_bg'][i], p['moe_we'][i],
                    p['moe_be'][i], p['moe_w13'][i], p['moe_w2'][i], nb=nb)


def kernel(x, c, ctx, c_ctx, ada_w, ada_b, norm_mix_w, norm_ffn_w, da_w_qkv, da_lambda, da_subln_w, da_w_o, dn_w_in, dn_conv_w, dn_a_log, dn_dt_bias, dn_norm_w, dn_w_o, wa_w_qkv, wa_sinks, wa_w_o, moe_wg, moe_bg, moe_we, moe_be, moe_w13, moe_w2, final_norm_w):
    p = dict(norm_mix_w=norm_mix_w, norm_ffn_w=norm_ffn_w, da_w_qkv=da_w_qkv,
             da_lambda=da_lambda, da_subln_w=da_subln_w, da_w_o=da_w_o, dn_w_in=dn_w_in,
             dn_conv_w=dn_conv_w, dn_a_log=dn_a_log, dn_dt_bias=dn_dt_bias, dn_norm_w=dn_norm_w,
             dn_w_o=dn_w_o, wa_w_qkv=wa_w_qkv, wa_sinks=wa_sinks, wa_w_o=wa_w_o, moe_wg=moe_wg,
             moe_bg=moe_bg, moe_we=moe_we, moe_be=moe_be, moe_w13=moe_w13, moe_w2=moe_w2)
    nb, seq, d = x.shape
    t = nb * (CTX_LEN + seq)
    s2d = jnp.concatenate([ctx, x], axis=1).reshape(t, d)
    cc = jnp.concatenate([c, c_ctx[None, :], jnp.zeros((16 - nb - 1, d), F32)], axis=0)
    mods = ada_tables(cc, ada_w, ada_b).reshape(DEPTH, 16, 1, 6 * d)
    rope = _rope_tables(seq)
    for i in range(DEPTH):
        s2d = token_mixer(i, s2d, mods[i], rope, p, nb=nb)
        s2d = moe_sublayer(i, s2d, mods[i], p, nb=nb)
    return final_norm(s2d, final_norm_w, nb=nb)
```

```python
import functools
import math

import jax
import jax.numpy as jnp
from jax import lax
from jax.experimental import pallas as pl
from jax.experimental.pallas import tpu as pltpu

F32 = jnp.float32
BF16 = jnp.bfloat16
HIGHEST = lax.Precision.HIGHEST

D_MODEL = 2048
DEPTH = 4
GRID_W = 64
CTX_LEN = 256
N_MIXERS = 3
EPS = 1e-6
ROPE_BASE = 10000.0
NEG_INF = -1e30
DA_HEADS = 16
DA_DIM = 64
DN_HK = 16
DN_HV = 32
DN_DK = 128
DN_CONV = 5
DN_CHUNK = 64
DN_QK = DN_HK * DN_DK
DN_VD = DN_HV * DN_DK
DN_QKV = 2 * DN_QK + DN_VD
WA_HQ = 32
WA_HKV = 4
WA_GROUP = 8
WA_DIM = 64
WINDOW = 128
N_GROUPS = 4
EXPERTS_PER_GROUP = 8
N_EXPERTS = 32
TOP_K = 2
D_EXPERT = 768

LANES = 128
TM = 256
BIG_TILES_PER_BATCH = 3
MOE_ROWS = 256
DN_G = 16
DA_TQ = 512
VMEM_LIMIT = 56 * 1024 * 1024


def _cparams(n_axes):
    return pltpu.CompilerParams(dimension_semantics=("arbitrary",) * n_axes,
                                vmem_limit_bytes=VMEM_LIMIT)


def _mm(a, b):
    return jnp.dot(a.astype(BF16), b.astype(BF16), preferred_element_type=F32)


def _mm_nt(a, b):
    return lax.dot_general(a.astype(BF16), b.astype(BF16), (((1,), (1,)), ((), ())),
                           preferred_element_type=F32)


def _split_bf16(a):
    hi = a.astype(BF16)
    return hi, (a - hi.astype(F32)).astype(BF16)


def _mm_presplit(a_parts, b_parts):
    b_hi, b_lo = b_parts
    m = a_parts[0].shape[0]
    o_hi = jnp.dot(jnp.concatenate(a_parts, axis=0), b_hi, preferred_element_type=F32)
    o_lo = jnp.dot(a_parts[0], b_lo, preferred_element_type=F32)
    return (o_hi[:m] + o_hi[m:]) + o_lo


def _sigmoid(x):
    return 1.0 / (1.0 + jnp.exp(-x))


def _mod_spec(nb, tpb, chunk, width):
    def imap(r, *_):
        return (jnp.where(r % tpb == 0, nb, r // tpb), 0, chunk)
    return pl.BlockSpec((1, 1, width), imap)


def _big_mod_specs(nb, tpb, chunk, width, col=None):
    def blk(j):
        return chunk if col is None else chunk * col + j
    return [pl.BlockSpec((1, 1, width), lambda r, j: (nb, 0, blk(j))),
            pl.BlockSpec((1, 1, width), lambda r, j: (r // tpb, 0, blk(j)))]


def _ctx_rows(rows, tiles_per_batch):
    first = (pl.program_id(0) % tiles_per_batch) * rows
    return first + lax.broadcasted_iota(jnp.int32, (rows, 1), 0) < CTX_LEN


def _ada_kernel(c_ref, w_ref, b_ref, o_ref):
    c = c_ref[...]
    o_ref[0] = jnp.dot(c * _sigmoid(c), w_ref[0], preferred_element_type=F32,
                       precision=HIGHEST) + b_ref[0]


def ada_tables(cc, ada_w, ada_b):
    depth, d, n = ada_w.shape
    tn = 768
    return pl.pallas_call(
        _ada_kernel,
        grid=(depth, n // tn),
        in_specs=[pl.BlockSpec((16, d), lambda l, j: (0, 0)),
                  pl.BlockSpec((1, d, tn), lambda l, j: (l, 0, j)),
                  pl.BlockSpec((1, 1, tn), lambda l, j: (l, 0, j))],
        out_specs=pl.BlockSpec((1, 16, tn), lambda l, j: (l, 0, j)),
        out_shape=jax.ShapeDtypeStruct((depth, 16, n), F32),
        compiler_params=_cparams(2),
        name="ada_tables",
    )(cc, ada_w, ada_b.reshape(depth, 1, n))


def _norm_mod(x, nw, sc, sh):
    y = x * lax.rsqrt(jnp.mean(x * x, axis=-1, keepdims=True) + EPS) * nw
    return y * (1.0 + sc) + sh


def _proj_kernel(*refs, epilogue, rope_tiles, tpb):
    if epilogue == "rope":
        (x_ref, nw_ref, scc_ref, scx_ref, shc_ref, shx_ref, w_ref, cos_ref, sa_ref, sb_ref,
         o_ref, h_scr) = refs
    else:
        x_ref, nw_ref, scc_ref, scx_ref, shc_ref, shx_ref, w_ref, o_ref, h_scr = refs
    j = pl.program_id(1)

    @pl.when(j == 0)
    def _():
        is_ctx = _ctx_rows(x_ref.shape[0], tpb)
        sc = jnp.where(is_ctx, scc_ref[0], scx_ref[0])
        sh = jnp.where(is_ctx, shc_ref[0], shx_ref[0])
        h_scr[...] = _norm_mod(x_ref[...], nw_ref[...], sc, sh).astype(h_scr.dtype)

    precision = HIGHEST if w_ref.dtype == F32 else None
    acc = jnp.dot(h_scr[...], w_ref[...], preferred_element_type=F32, precision=precision)
    if epilogue == "rope":
        @pl.when(j < rope_tiles)
        def _():
            cos, sa, sb = cos_ref[...], sa_ref[...], sb_ref[...]
            for g in range(acc.shape[1] // LANES):
                a = acc[:, g * LANES:(g + 1) * LANES]
                rot = a * cos + pltpu.roll(a, LANES - 16, 1) * sa + pltpu.roll(a, 16, 1) * sb
                o_ref[:, g * LANES:(g + 1) * LANES] = rot.astype(o_ref.dtype)

        @pl.when(j >= rope_tiles)
        def _():
            o_ref[...] = acc.astype(o_ref.dtype)
    elif epilogue == "silu":
        o_ref[...] = (acc * _sigmoid(acc)).astype(o_ref.dtype)
    else:
        o_ref[...] = acc.astype(o_ref.dtype)


def proj(s2d, nw, mod, w, *, nb, tn, out_dtype, epilogue=None, rope=None, rope_tiles=0):
    t, d = s2d.shape
    n = w.shape[1]
    tpb = BIG_TILES_PER_BATCH
    tm = t // nb // tpb
    in_specs = ([pl.BlockSpec((tm, d), lambda r, j: (r, 0)),
                 pl.BlockSpec((1, d), lambda r, j: (0, 0))]
                + _big_mod_specs(nb, tpb, 1, d) + _big_mod_specs(nb, tpb, 0, d)
                + [pl.BlockSpec((d, tn), lambda r, j: (0, j))])
    args = [s2d, nw.reshape(1, d), mod, mod, mod, mod, w]
    if epilogue == "rope":
        tab = pl.BlockSpec((tm, LANES), lambda r, j: (r % tpb, 0))
        in_specs += [tab, tab, tab]
        args += list(rope)
    return pl.pallas_call(
        functools.partial(_proj_kernel, epilogue=epilogue, rope_tiles=rope_tiles, tpb=tpb),
        grid=(t // tm, n // tn),
        in_specs=in_specs,
        out_specs=pl.BlockSpec((tm, tn), lambda r, j: (r, j)),
        out_shape=jax.ShapeDtypeStruct((t, n), out_dtype),
        scratch_shapes=[pltpu.VMEM((tm, d), w.dtype)],
        compiler_params=_cparams(2),
        name="proj",
    )(*args)


def _out_kernel(a_ref, w_ref, gc_ref, gx_ref, s_ref, o_ref, *, tpb):
    acc = jnp.dot(a_ref[...], w_ref[...], preferred_element_type=F32)
    gate = jnp.where(_ctx_rows(a_ref.shape[0], tpb), gc_ref[0], gx_ref[0])
    o_ref[...] = s_ref[...] + gate * acc


def out_proj(a, w, mod, s2d, *, nb, gate_chunk, tn=512):
    t, k = a.shape
    d = w.shape[1]
    tpb = BIG_TILES_PER_BATCH
    tm = t // nb // tpb
    return pl.pallas_call(
        functools.partial(_out_kernel, tpb=tpb),
        grid=(t // tm, d // tn),
        in_specs=[pl.BlockSpec((tm, k), lambda r, j: (r, 0)),
                  pl.BlockSpec((k, tn), lambda r, j: (0, j))]
                 + _big_mod_specs(nb, tpb, gate_chunk, tn, col=d // tn)
                 + [pl.BlockSpec((tm, tn), lambda r, j: (r, j))],
        out_specs=pl.BlockSpec((tm, tn), lambda r, j: (r, j)),
        out_shape=jax.ShapeDtypeStruct((t, d), F32),
        compiler_params=_cparams(2),
        name="out_proj",
    )(a, w, mod, mod, s2d)


def _da_kernel(lam_ref, q_ref, k_ref, v_ref, sw_ref, o_ref, klo, khi, *, lambda_init, n_ctx, tq):
    lam = lam_ref[...]
    l01 = jnp.sum(lam[0:1] * lam[1:2], axis=-1, keepdims=True)
    l23 = jnp.sum(lam[2:3] * lam[3:4], axis=-1, keepdims=True)
    lam_full = jnp.exp(l01) - jnp.exp(l23) + lambda_init
    lane = lax.broadcasted_iota(jnp.int32, (1, LANES), 1)
    k = k_ref[...]
    zero = jnp.zeros_like(k)
    klo[...] = jnp.where(lane < DA_DIM, k, zero)
    khi[...] = jnp.where(lane >= DA_DIM, k, zero)

    def attend(row0, rows, nk):
        q = q_ref[pl.ds(row0, rows), :] * (DA_DIM ** -0.5)
        v = v_ref[0:nk, :]
        s1 = _mm_nt(q, klo[0:nk, :])
        s2 = _mm_nt(q, khi[0:nk, :])
        e1 = jnp.exp(s1 - jnp.max(s1, axis=-1, keepdims=True))
        e2 = jnp.exp(s2 - jnp.max(s2, axis=-1, keepdims=True))
        o1 = _mm(e1, v) / jnp.sum(e1, axis=-1, keepdims=True)
        o2 = _mm(e2, v) / jnp.sum(e2, axis=-1, keepdims=True)
        o = o1 - lam_full * o2
        o = o * lax.rsqrt(jnp.mean(o * o, axis=-1, keepdims=True) + EPS) * sw_ref[...]
        o_ref[pl.ds(row0, rows), :] = (o * (1.0 - lambda_init)).astype(o_ref.dtype)

    attend(0, n_ctx, n_ctx)

    def latent_tile(i, carry):
        attend(pl.multiple_of(n_ctx + i * tq, LANES), tq, k_ref.shape[0])
        return carry
    lax.fori_loop(0, (q_ref.shape[0] - n_ctx) // tq, latent_tile, 0)


def diff_attention(qkv, lam, subln_w, *, nb, lambda_init):
    t = qkv.shape[0]
    lt = t // nb
    col = pl.BlockSpec((lt, LANES), lambda b, h: (b, h))
    return pl.pallas_call(
        functools.partial(_da_kernel, lambda_init=lambda_init, n_ctx=CTX_LEN, tq=DA_TQ),
        grid=(nb, DA_HEADS),
        in_specs=[pl.BlockSpec((4, DA_DIM), lambda b, h: (0, 0)),
                  col,
                  pl.BlockSpec((lt, LANES), lambda b, h: (b, DA_HEADS + h)),
                  pl.BlockSpec((lt, LANES), lambda b, h: (b, 2 * DA_HEADS + h)),
                  pl.BlockSpec((1, LANES), lambda b, h: (0, 0))],
        out_specs=col,
        out_shape=jax.ShapeDtypeStruct((t, D_MODEL), BF16),
        scratch_shapes=[pltpu.VMEM((lt, LANES), BF16)] * 2,
        compiler_params=_cparams(2),
        name="diff_attention",
    )(lam, qkv, qkv, qkv, subln_w.reshape(1, LANES))


def _wa_kernel(sink_ref, q_ref, k_ref, v_ref, o_ref, klo, khi, vlo, vhi, *, n_ctx, seq):
    hkv = pl.program_id(1)
    i = pl.program_id(2)
    tq = q_ref.shape[0]
    band = tq + 2 * WINDOW
    npair = WA_GROUP // 2
    lane = lax.broadcasted_iota(jnp.int32, (1, LANES), 1)
    n_ctx_tiles = n_ctx // tq

    @pl.when(i == 0)
    def _():
        odd = (hkv % 2) == 1
        for src, lo, hi in ((k_ref, klo, khi), (v_ref, vlo, vhi)):
            x = src[...].astype(F32)
            x = jnp.where(odd, pltpu.roll(x, WA_DIM, 1), x)
            x = jnp.where(lane < WA_DIM, x, 0.0)
            lo[...] = x.astype(BF16)
            hi[...] = pltpu.roll(x, WA_DIM, 1).astype(BF16)

    def attend(with_band):
        halves = 2
        hp = npair // halves
        rows = hp * tq
        qs = [jnp.concatenate([q_ref[:, p * LANES:(p + 1) * LANES]
                               for p in range(c * hp, (c + 1) * hp)], axis=0) * (WA_DIM ** -0.5)
              for c in range(halves)]
        chains = [(c, par) for c in range(halves) for par in range(2)]
        kv = ((klo, vlo), (khi, vhi))
        sink = [jnp.concatenate([jnp.full((tq, 1), sink_ref[hkv, 2 * p + par], F32)
                                 for p in range(c * hp, (c + 1) * hp)], axis=0)
                for c, par in chains]
        s_c = [_mm_nt(qs[c], kv[par][0][0:n_ctx, :]) for c, par in chains]
        m = [jnp.maximum(jnp.max(s_c[n], axis=-1, keepdims=True), sink[n])
             for n in range(len(chains))]
        if with_band:
            lat0 = (i - n_ctx_tiles) * tq
            start = jnp.clip(lat0 - WINDOW, 0, seq - band)
            row0 = pl.multiple_of(n_ctx + start, LANES)
            qpos = lat0 + (lax.broadcasted_iota(jnp.int32, (rows, band), 0) & (tq - 1))
            kpos = start + lax.broadcasted_iota(jnp.int32, (rows, band), 1)
            valid = jnp.abs(qpos - kpos) <= WINDOW
            s_b = [jnp.where(valid, _mm_nt(qs[c], kv[par][0][pl.ds(row0, band), :]), NEG_INF)
                   for c, par in chains]
            m = [jnp.maximum(m[n], jnp.max(s_b[n], axis=-1, keepdims=True))
                 for n in range(len(chains))]
            e_b = [jnp.exp(s_b[n] - m[n]) for n in range(len(chains))]
        e_c = [jnp.exp(s_c[n] - m[n]) for n in range(len(chains))]
        den = [jnp.sum(e_c[n], axis=-1, keepdims=True) + jnp.exp(sink[n] - m[n])
               for n in range(len(chains))]
        acc = [_mm(e_c[n], kv[par][1][0:n_ctx, :]) for n, (c, par) in enumerate(chains)]
        if with_band:
            den = [den[n] + jnp.sum(e_b[n], axis=-1, keepdims=True) for n in range(len(chains))]
            acc = [acc[n] + _mm(e_b[n], kv[par][1][pl.ds(row0, band), :])
                   for n, (c, par) in enumerate(chains)]
        out = [acc[2 * c] / den[2 * c] + acc[2 * c + 1] / den[2 * c + 1] for c in range(halves)]
        for p in range(npair):
            c, r = divmod(p, hp)
            o_ref[:, p * LANES:(p + 1) * LANES] = out[c][r * tq:(r + 1) * tq].astype(o_ref.dtype)

    @pl.when(i < n_ctx_tiles)
    def _():
        attend(False)

    @pl.when(i >= n_ctx_tiles)
    def _():
        attend(True)


def window_attention(qkv, sinks, *, nb):
    t = qkv.shape[0]
    lt = t // nb
    tq = 128
    nq = lt // tq
    qw = WA_GROUP * WA_DIM
    kblk = WA_HQ * WA_DIM // LANES
    vblk = kblk + WA_HKV * WA_DIM // LANES
    return pl.pallas_call(
        functools.partial(_wa_kernel, n_ctx=CTX_LEN, seq=lt - CTX_LEN),
        grid=(nb, WA_HKV, nq),
        in_specs=[pl.BlockSpec(memory_space=pltpu.SMEM),
                  pl.BlockSpec((tq, qw), lambda b, h, i: (b * nq + i, h)),
                  pl.BlockSpec((lt, LANES), lambda b, h, i: (b, kblk + h // 2)),
                  pl.BlockSpec((lt, LANES), lambda b, h, i: (b, vblk + h // 2))],
        out_specs=pl.BlockSpec((tq, qw), lambda b, h, i: (b * nq + i, h)),
        out_shape=jax.ShapeDtypeStruct((t, D_MODEL), BF16),
        scratch_shapes=[pltpu.VMEM((lt, LANES), BF16)] * 4,
        compiler_params=_cparams(3),
        name="window_attention",
    )(sinks.reshape(WA_HKV, WA_GROUP), qkv, qkv, qkv)


def _dn_conv_kernel(x_ref, w_ref, o_ref, *, n_ctx):
    cb = pl.program_id(1)
    x = x_ref[...]
    n = x.shape[0]
    w = w_ref[...]
    t = lax.broadcasted_iota(jnp.int32, (n, 1), 0)
    lo = jnp.where(t < n_ctx, 0, n_ctx)
    hi = jnp.where(t < n_ctx, n_ctx, n)
    acc = jnp.zeros_like(x)
    for j in range(DN_CONV):
        sft = j - (DN_CONV - 1) // 2
        xs = x if sft == 0 else pltpu.roll(x, (-sft) % n, 0)
        u = t + sft
        ok = (u >= lo) & (u < hi)
        acc = acc + jnp.where(ok, xs, 0.0) * w[j:j + 1, :]
    y = acc * _sigmoid(acc)
    nrm = lax.rsqrt(jnp.sum(y * y, axis=-1, keepdims=True) + EPS)
    scl = jnp.where(cb < DN_HK, nrm * (DN_DK ** -0.5), jnp.where(cb < 2 * DN_HK, nrm, 1.0))
    o_ref[...] = y * scl


def dn_conv(qkv_pre, conv_w, *, nb):
    t, n = qkv_pre.shape
    lt = t // nb
    return pl.pallas_call(
        functools.partial(_dn_conv_kernel, n_ctx=CTX_LEN),
        grid=(nb, n // LANES),
        in_specs=[pl.BlockSpec((lt, LANES), lambda b, c: (b, c)),
                  pl.BlockSpec((DN_CONV, LANES), lambda b, c: (0, c))],
        out_specs=pl.BlockSpec((lt, LANES), lambda b, c: (b, c)),
        out_shape=jax.ShapeDtypeStruct((t, n), F32),
        compiler_params=_cparams(2),
        name="dn_conv",
    )(qkv_pre, conv_w)


def _dn_gate_kernel(ba_ref, alog_ref, dtb_ref, beta_ref, gc_ref, gt_ref):
    n = ba_ref.shape[0]
    c = DN_CHUNK
    ba = ba_ref[...]
    beta_ref[...] = _sigmoid(ba)
    z = ba + dtb_ref[...]
    softplus = jnp.maximum(z, 0.0) + jnp.log(1.0 + jnp.exp(-jnp.abs(z)))
    g = -jnp.exp(alog_ref[...]) * softplus
    row = lax.broadcasted_iota(jnp.int32, (c, c), 0)
    col = lax.broadcasted_iota(jnp.int32, (c, c), 1)
    prefix = (col <= row).astype(F32)
    suffix = (col >= row).astype(F32)
    ones = jnp.ones((c, c), F32)
    fwd_lane = lax.broadcasted_iota(jnp.int32, (1, LANES), 1) < 2 * DN_HV
    for ci in range(n // c):
        gch = g[ci * c:(ci + 1) * c, :]
        csum = jnp.where(fwd_lane,
                         jnp.dot(prefix, gch, preferred_element_type=F32, precision=HIGHEST),
                         jnp.dot(suffix, gch, preferred_element_type=F32, precision=HIGHEST))
        gc_ref[ci * c:(ci + 1) * c, :] = csum
        gt_ref[ci * c:(ci + 1) * c, :] = jnp.dot(ones, gch, preferred_element_type=F32,
                                                 precision=HIGHEST)


def dn_gates(ba, a_log, dt_bias, *, nb):
    t = ba.shape[0]
    lt = t // nb
    zeros = jnp.zeros((DN_HV,), F32)
    alog = jnp.concatenate([zeros, a_log[0], zeros, a_log[1]]).reshape(1, LANES)
    dtb = jnp.concatenate([zeros, dt_bias[0], zeros, dt_bias[1]]).reshape(1, LANES)
    blk = pl.BlockSpec((lt, LANES), lambda b: (b, 0))
    vec = pl.BlockSpec((1, LANES), lambda b: (0, 0))
    shp = jax.ShapeDtypeStruct((t, LANES), F32)
    return pl.pallas_call(
        _dn_gate_kernel,
        grid=(nb,),
        in_specs=[blk, vec, vec],
        out_specs=[blk, blk, blk],
        out_shape=[shp, shp, shp],
        compiler_params=_cparams(1),
        name="dn_gates",
    )(ba, alog, dtb)


def _dn_chunk_index(d, c, n_ctx_chunks, n_chunks):
    bwd = jnp.where(c < n_ctx_chunks, n_ctx_chunks - 1 - c, n_chunks - 1 + n_ctx_chunks - c)
    return jnp.where(d == 0, c, bwd)


def _dn_delta_kernel(q_ref, k_ref, v_ref, gcol_ref, grow_ref, o_ref, s_scr):
    d = pl.program_id(2)
    c = pl.program_id(3)
    cs = DN_CHUNK

    @pl.when(c == 0)
    def _():
        s_scr[...] = jnp.zeros_like(s_scr)

    row = lax.broadcasted_iota(jnp.int32, (cs, cs), 0)
    col = lax.broadcasted_iota(jnp.int32, (cs, cs), 1)
    ahead = (row - col) * jnp.where(d == 0, 1, -1)
    incl = ahead >= 0
    strict = ahead > 0
    eye = (row == col).astype(F32)

    gates = gcol_ref[0, 0, 0]
    grows = grow_ref[0, 0, 0, 0]
    heads = range(DN_G)
    qs = [q_ref[:, hk * DN_DK:(hk + 1) * DN_DK] for hk in range(DN_G // 2)]
    ks = [k_ref[:, hk * DN_DK:(hk + 1) * DN_DK] for hk in range(DN_G // 2)]
    vs = [v_ref[:, h * DN_DK:(h + 1) * DN_DK] for h in heads]
    kks = [_mm_nt(k, k) for k in ks]
    qks = [_mm_nt(q, k) for q, k in zip(qs, ks)]
    beta = [gates[:, h:h + 1] for h in heads]
    gc = [gates[:, DN_G + h:DN_G + h + 1] for h in heads]
    gt = [gates[:, 2 * DN_G + h:2 * DN_G + h + 1] for h in heads]
    decay = [jnp.exp(jnp.where(incl, gc[h] - grows[h:h + 1, :], NEG_INF)) for h in heads]
    pw = [-jnp.where(strict, kks[h // 2] * (beta[h] * decay[h]), 0.0) for h in heads]
    tinv = [eye + pw[h] for h in heads]
    sp = [_split_bf16(pw[h]) for h in heads]
    for _ in range(5):
        sp = [_split_bf16(_mm_presplit(sp[h], sp[h])) for h in heads]
        tinv = [tinv[h] + _mm_presplit(_split_bf16(tinv[h]), sp[h]) for h in heads]
    eg = [jnp.exp(gc[h]) for h in heads]
    uw = [_mm(tinv[h], jnp.concatenate([vs[h] * beta[h], ks[h // 2] * (beta[h] * eg[h])], axis=1))
          for h in heads]
    s_old = [s_scr[h] for h in heads]
    v_new = [uw[h][:, :DN_DK] - _mm(uw[h][:, DN_DK:], s_old[h]) for h in heads]
    outs = [_mm(qs[h // 2] * eg[h], s_old[h])
            + _mm(jnp.where(incl, qks[h // 2] * decay[h], 0.0), v_new[h]) for h in heads]
    o_ref[0] = jnp.concatenate(outs, axis=1)
    kd = [ks[h // 2] * jnp.exp(gt[h] - gc[h]) for h in heads]
    s_scr[...] = jnp.stack([s_old[h] * jnp.exp(gt[h][0:1, :]) + _mm(kd[h].T, v_new[h])
                            for h in heads], axis=0)


def dn_delta(qkv, gcol, grow, *, nb):
    t = qkv.shape[0]
    lt = t // nb
    nch = lt // DN_CHUNK
    nctx = CTX_LEN // DN_CHUNK
    ng = DN_HV // DN_G
    qw = DN_G // 2 * DN_DK
    vw = DN_G * DN_DK
    cidx = functools.partial(_dn_chunk_index, n_ctx_chunks=nctx, n_chunks=nch)
    return pl.pallas_call(
        _dn_delta_kernel,
        grid=(nb, ng, 2, nch),
        in_specs=[pl.BlockSpec((DN_CHUNK, qw), lambda b, g, d, c: (b * nch + cidx(d, c), g)),
                  pl.BlockSpec((DN_CHUNK, qw), lambda b, g, d, c: (b * nch + cidx(d, c),
                                                                   DN_QK // qw + g)),
                  pl.BlockSpec((DN_CHUNK, vw), lambda b, g, d, c: (b * nch + cidx(d, c),
                                                                   2 * DN_QK // vw + g)),
                  pl.BlockSpec((1, 1, 1, DN_CHUNK, 3 * DN_G),
                               lambda b, g, d, c: (b, d, g, cidx(d, c), 0)),
                  pl.BlockSpec((1, 1, 1, 1, DN_G, DN_CHUNK),
                               lambda b, g, d, c: (b, d, g, cidx(d, c), 0, 0))],
        out_specs=pl.BlockSpec((1, DN_CHUNK, vw), lambda b, g, d, c: (d, b * nch + cidx(d, c), g)),
        out_shape=jax.ShapeDtypeStruct((2, t, DN_VD), F32),
        scratch_shapes=[pltpu.VMEM((DN_G, DN_DK, DN_DK), F32)],
        compiler_params=_cparams(4),
        name="dn_delta",
    )(qkv, qkv, qkv, gcol, grow)


def _dn_gate_layouts(beta, gc, gt, *, nb):
    t = beta.shape[0]
    lt = t // nb
    nch = lt // DN_CHUNK
    ng = DN_HV // DN_G
    cols, rows = [], []
    for d in range(2):
        b0 = 2 * d * DN_HV
        parts = [beta[:, b0:b0 + DN_HV], gc[:, b0 + DN_HV:b0 + 2 * DN_HV],
                 gt[:, b0 + DN_HV:b0 + 2 * DN_HV]]
        parts = [p.reshape(nb, lt, ng, DN_G).transpose(0, 2, 1, 3) for p in parts]
        cols.append(jnp.concatenate(parts, axis=-1))
        rows.append(parts[1].reshape(nb, ng, nch, DN_CHUNK, DN_G).transpose(0, 1, 2, 4, 3))
    return jnp.stack(cols, axis=1), jnp.stack(rows, axis=1)


def _dn_gatenorm_kernel(o_ref, zs_ref, nw_ref, a_ref):
    for h in range(DN_HV):
        sl = slice(h * DN_DK, (h + 1) * DN_DK)
        o = o_ref[0, :, sl] + o_ref[1, :, sl]
        y = o * lax.rsqrt(jnp.mean(o * o, axis=-1, keepdims=True) + EPS) * nw_ref[...]
        a_ref[:, sl] = (y * zs_ref[:, sl].astype(F32)).astype(a_ref.dtype)


def dn_gatenorm(o2, zs, norm_w):
    _, t, k = o2.shape
    return pl.pallas_call(
        _dn_gatenorm_kernel,
        grid=(t // TM,),
        in_specs=[pl.BlockSpec((2, TM, k), lambda r: (0, r, 0)),
                  pl.BlockSpec((TM, k), lambda r: (r, 0)),
                  pl.BlockSpec((1, DN_DK), lambda r: (0, 0))],
        out_specs=pl.BlockSpec((TM, k), lambda r: (r, 0)),
        out_shape=jax.ShapeDtypeStruct((t, k), BF16),
        compiler_params=_cparams(1),
        name="dn_gatenorm",
    )(o2, zs, norm_w.reshape(1, DN_DK))


def _router_kernel(x_ref, nw_ref, sc_ref, sh_ref, wr_ref, br_ref, hp_ref, eid_ref, gate_ref,
                   cnt_ref):
    h = _norm_mod(x_ref[...], nw_ref[...], sc_ref[0], sh_ref[0])
    logits = jnp.dot(h, wr_ref[...], preferred_element_type=F32, precision=HIGHEST) + br_ref[...]
    bits = pltpu.bitcast(h.astype(BF16).astype(F32), jnp.uint32)
    half = bits.shape[1] // 2
    hp_ref[...] = bits[:, :half] | (bits[:, half:] >> 16)
    lane = lax.broadcasted_iota(jnp.int32, logits.shape, 1)
    is_grp = lane < N_GROUPS
    mg = jnp.max(jnp.where(is_grp, logits, -jnp.inf), axis=-1, keepdims=True)
    eg = jnp.where(is_grp, jnp.exp(logits - mg), 0.0)
    pg = eg / jnp.sum(eg, axis=-1, keepdims=True)
    p_grp = jnp.max(pg, axis=-1, keepdims=True)
    g_sel = jnp.min(jnp.where(is_grp & (pg == p_grp), lane, LANES), axis=-1, keepdims=True)
    fine = lane - N_GROUPS
    in_grp = (fine >= 0) & (fine < N_EXPERTS) & ((fine // EXPERTS_PER_GROUP) == g_sel)
    mf = jnp.max(jnp.where(in_grp, logits, -jnp.inf), axis=-1, keepdims=True)
    ef = jnp.where(in_grp, jnp.exp(logits - mf), 0.0)
    pf = ef / jnp.sum(ef, axis=-1, keepdims=True)
    p1 = jnp.max(jnp.where(in_grp, pf, -1.0), axis=-1, keepdims=True)
    i1 = jnp.min(jnp.where(in_grp & (pf == p1), lane, LANES), axis=-1, keepdims=True)
    rest = in_grp & (lane != i1)
    p2 = jnp.max(jnp.where(rest, pf, -1.0), axis=-1, keepdims=True)
    i2 = jnp.min(jnp.where(rest & (pf == p2), lane, LANES), axis=-1, keepdims=True)
    den = p1 + p2
    gate_ref[...] = jnp.where(lane == 0, p_grp * p1 / den,
                              jnp.where(lane == 1, p_grp * p2 / den, 0.0))
    e1 = i1 - N_GROUPS
    e2 = i2 - N_GROUPS
    oh1 = jnp.where(lane == e1, 1.0, 0.0)
    oh2 = jnp.where(lane == e2, 1.0, 0.0)
    both = oh1 + oh2
    rows = both.shape[0]
    earlier = (lax.broadcasted_iota(jnp.int32, (rows, rows), 1)
               < lax.broadcasted_iota(jnp.int32, (rows, rows), 0))
    seen = jnp.dot(jnp.where(earlier, 1.0, 0.0).astype(BF16), both.astype(BF16),
                   preferred_element_type=F32)
    r1 = jnp.sum(seen * oh1, axis=-1, keepdims=True).astype(jnp.int32)
    r2 = jnp.sum(seen * oh2, axis=-1, keepdims=True).astype(jnp.int32)
    eid_ref[...] = jnp.where(lane == 0, e1, jnp.where(lane == 1, e2, jnp.where(
        lane == 2, r1, jnp.where(lane == 3, r2, 0))))
    cnt_ref[0] = jnp.broadcast_to(jnp.sum(both, axis=0, keepdims=True), (8, LANES)).astype(jnp.int32)


def moe_router(s2d, nw, mod, wr, br, *, nb, sc_chunk, sh_chunk):
    t, d = s2d.shape
    tpb = t // nb // TM
    row = pl.BlockSpec((TM, d), lambda r: (r, 0))
    small = pl.BlockSpec((TM, LANES), lambda r: (r, 0))
    return pl.pallas_call(
        _router_kernel,
        grid=(t // TM,),
        in_specs=[row, pl.BlockSpec((1, d), lambda r: (0, 0)),
                  _mod_spec(nb, tpb, sc_chunk, d), _mod_spec(nb, tpb, sh_chunk, d),
                  pl.BlockSpec((d, LANES), lambda r: (0, 0)),
                  pl.BlockSpec((1, LANES), lambda r: (0, 0))],
        out_specs=[pl.BlockSpec((TM, d // 2), lambda r: (r, 0)), small, small,
                   pl.BlockSpec((1, 8, LANES), lambda r: (r, 0, 0))],
        out_shape=[jax.ShapeDtypeStruct((t, d // 2), jnp.uint32),
                   jax.ShapeDtypeStruct((t, LANES), jnp.int32),
                   jax.ShapeDtypeStruct((t, LANES), F32),
                   jax.ShapeDtypeStruct((t // TM, 8, LANES), jnp.int32)],
        compiler_params=_cparams(1),
        name="moe_router",
    )(s2d, nw.reshape(1, d), mod, mod, wr, br)


def _row_copy(src_hbm, row, dst, r, sem):
    return pltpu.make_async_copy(src_hbm.at[pl.ds(row, 1), :], dst.at[pl.ds(r, 1), :], sem)


def _moe_gather_kernel(tok_ref, nused_ref, h_hbm, o_ref, buf, sem):
    blk = pl.program_id(0)
    rows = buf.shape[1]
    n_used = nused_ref[0]

    def fetch(b):
        slot = b % 2

        def issue(r, carry):
            _row_copy(h_hbm, tok_ref[b * rows + r], buf.at[slot], r, sem.at[slot]).start()
            return carry
        lax.fori_loop(0, rows, issue, 0, unroll=8)

    @pl.when(blk == 0)
    def _():
        fetch(blk)

    @pl.when(blk + 1 < n_used)
    def _():
        fetch(blk + 1)

    @pl.when(blk < n_used)
    def _():
        slot = blk % 2

        def wait(r, carry):
            _row_copy(h_hbm, 0, buf.at[slot], r, sem.at[slot]).wait()
            return carry
        lax.fori_loop(0, rows, wait, 0, unroll=8)
        o_ref[...] = buf[slot]

    @pl.when(blk >= n_used)
    def _():
        o_ref[...] = jnp.zeros_like(o_ref)


def moe_gather(h, buf_tok, n_used):
    t, d = h.shape
    p = buf_tok.shape[0]
    return pl.pallas_call(
        _moe_gather_kernel,
        grid_spec=pltpu.PrefetchScalarGridSpec(
            num_scalar_prefetch=2,
            grid=(p // MOE_ROWS,),
            in_specs=[pl.BlockSpec(memory_space=pl.ANY)],
            out_specs=pl.BlockSpec((MOE_ROWS, d), lambda i, *_: (i, 0)),
            scratch_shapes=[pltpu.VMEM((2, MOE_ROWS, d), h.dtype), pltpu.SemaphoreType.DMA((2,))]),
        out_shape=jax.ShapeDtypeStruct((p, d), h.dtype),
        compiler_params=_cparams(1),
        name="moe_gather",
    )(buf_tok, n_used, h)


def _moe_expert_kernel(be_ref, nused_ref, x_ref, w13_ref, w2_ref, o_ref):
    blk = pl.program_id(0)

    @pl.when(blk < nused_ref[0])
    def _():
        packed = x_ref[...]
        x = jnp.concatenate([pltpu.bitcast(packed & jnp.uint32(0xFFFF0000), F32),
                             pltpu.bitcast(packed << 16, F32)], axis=1).astype(BF16)
        h13 = jnp.dot(x, w13_ref[0, 0], preferred_element_type=F32)
        gate = h13[:, :D_EXPERT]
        act = gate * _sigmoid(gate) * h13[:, D_EXPERT:]
        o_ref[...] = jnp.dot(act.astype(BF16), w2_ref[0, 0], preferred_element_type=F32)

    @pl.when(blk >= nused_ref[0])
    def _():
        o_ref[...] = jnp.zeros_like(o_ref)


def moe_experts(xs, block_e, n_used, w13, w2, layer):
    p = xs.shape[0]
    d = w13.shape[2]
    return pl.pallas_call(
        _moe_expert_kernel,
        grid_spec=pltpu.PrefetchScalarGridSpec(
            num_scalar_prefetch=2,
            grid=(p // MOE_ROWS,),
            in_specs=[pl.BlockSpec((MOE_ROWS, d // 2), lambda i, be, nu: (i, 0)),
                      pl.BlockSpec((1, 1, d, 2 * D_EXPERT), lambda i, be, nu: (layer, be[i], 0, 0)),
                      pl.BlockSpec((1, 1, D_EXPERT, d), lambda i, be, nu: (layer, be[i], 0, 0))],
            out_specs=pl.BlockSpec((MOE_ROWS, d), lambda i, be, nu: (i, 0))),
        out_shape=jax.ShapeDtypeStruct((p, d), F32),
        compiler_params=_cparams(1),
        name="moe_experts",
    )(block_e, n_used, xs, w13, w2)


def _moe_combine_kernel(d0_ref, d1_ref, y_hbm, s_ref, gate_ref, g_ref, o_ref, buf, sem):
    i = pl.program_id(0)
    rows = buf.shape[2]

    def fetch(tile):
        slot = tile % 2

        def issue(r, carry):
            _row_copy(y_hbm, d0_ref[tile * rows + r], buf.at[slot, 0], r, sem.at[slot, 0]).start()
            _row_copy(y_hbm, d1_ref[tile * rows + r], buf.at[slot, 1], r, sem.at[slot, 1]).start()
            return carry
        lax.fori_loop(0, rows, issue, 0, unroll=8)

    @pl.when(i == 0)
    def _():
        fetch(i)

    @pl.when(i + 1 < pl.num_programs(0))
    def _():
        fetch(i + 1)

    slot = i % 2

    def wait(r, carry):
        _row_copy(y_hbm, 0, buf.at[slot, 0], r, sem.at[slot, 0]).wait()
        _row_copy(y_hbm, 0, buf.at[slot, 1], r, sem.at[slot, 1]).wait()
        return carry
    lax.fori_loop(0, rows, wait, 0, unroll=8)
    gates = gate_ref[...]
    y = gates[:, 0:1] * buf[slot, 0] + gates[:, 1:2] * buf[slot, 1]
    o_ref[...] = s_ref[...] + g_ref[0] * y


def moe_combine(yb, dest0, dest1, s2d, gates, mod, *, nb, gate_chunk):
    t, d = s2d.shape
    tpb = t // nb // TM
    return pl.pallas_call(
        _moe_combine_kernel,
        grid_spec=pltpu.PrefetchScalarGridSpec(
            num_scalar_prefetch=2,
            grid=(t // TM,),
            in_specs=[pl.BlockSpec(memory_space=pl.ANY),
                      pl.BlockSpec((TM, d), lambda r, *_: (r, 0)),
                      pl.BlockSpec((TM, LANES), lambda r, *_: (r, 0)),
                      _mod_spec(nb, tpb, gate_chunk, d)],
            out_specs=pl.BlockSpec((TM, d), lambda r, *_: (r, 0)),
            scratch_shapes=[pltpu.VMEM((2, 2, TM, d), F32), pltpu.SemaphoreType.DMA((2, 2))]),
        out_shape=jax.ShapeDtypeStruct((t, d), F32),
        compiler_params=_cparams(1),
        name="moe_combine",
    )(dest0, dest1, yb, s2d, gates, mod)


def _moe_plan(eid, tile_counts):
    t = eid.shape[0]
    tk = t * TOP_K
    experts = eid[:, :TOP_K]
    in_tile_rank = eid[:, TOP_K:2 * TOP_K]
    per_tile = tile_counts[:, 0, :N_EXPERTS]
    counts = jnp.sum(per_tile, axis=0)
    padded = (counts + MOE_ROWS - 1) // MOE_ROWS * MOE_ROWS
    pend = jnp.cumsum(padded)
    base = (pend - padded)[None, :] + jnp.cumsum(per_tile, axis=0) - per_tile
    tile = jnp.arange(t, dtype=jnp.int32) // TM
    dest = base[tile[:, None], experts] + in_tile_rank
    n_blocks = tk // MOE_ROWS + N_EXPERTS
    buf_tok = jnp.zeros((n_blocks * MOE_ROWS,), jnp.int32).at[dest.reshape(tk)].set(
        jnp.arange(tk, dtype=jnp.int32) // TOP_K)
    block_e = jnp.searchsorted(pend, jnp.arange(n_blocks, dtype=jnp.int32) * MOE_ROWS, side='right')
    block_e = jnp.minimum(block_e, N_EXPERTS - 1).astype(jnp.int32)
    n_used = (pend[-1:] // MOE_ROWS).astype(jnp.int32)
    return buf_tok, block_e, n_used, dest[:, 0], dest[:, 1]


def hier_moe(s2d, nw, mod, wg, bg, we, be, w13, w2, layer, *, nb):
    d = s2d.shape[1]
    pad = LANES - N_GROUPS - N_EXPERTS
    wr = jnp.concatenate([wg, we, jnp.zeros((d, pad), F32)], axis=1)
    br = jnp.concatenate([bg, be, jnp.zeros((pad,), F32)]).reshape(1, LANES)
    hp, eid, gates, tile_counts = moe_router(s2d, nw, mod, wr, br, nb=nb, sc_chunk=4, sh_chunk=3)
    buf_tok, block_e, n_used, dest0, dest1 = _moe_plan(eid, tile_counts)
    xs = moe_gather(hp, buf_tok, n_used)
    yb = moe_experts(xs, block_e, n_used, w13, w2, layer)
    return moe_combine(yb, dest0, dest1, s2d, gates, mod, nb=nb, gate_chunk=5)


def _final_norm_kernel(x_ref, w_ref, o_ref):
    x = x_ref[...]
    o_ref[0] = x * lax.rsqrt(jnp.mean(x * x, axis=-1, keepdims=True) + EPS) * w_ref[...]


def final_norm(s2d, w, *, nb):
    t, d = s2d.shape
    lt = t // nb
    tpb = lt // TM
    ctx_tiles = CTX_LEN // TM
    return pl.pallas_call(
        _final_norm_kernel,
        grid=(nb, tpb - ctx_tiles),
        in_specs=[pl.BlockSpec((TM, d), lambda b, i: (b * tpb + ctx_tiles + i, 0)),
                  pl.BlockSpec((1, d), lambda b, i: (0, 0))],
        out_specs=pl.BlockSpec((1, TM, d), lambda b, i: (b, i, 0)),
        out_shape=jax.ShapeDtypeStruct((nb, lt - CTX_LEN, d), F32),
        compiler_params=_cparams(2),
        name="final_norm",
    )(s2d, w.reshape(1, d))


def _rope_tables(seq):
    dim = DA_DIM
    half = dim // 2
    rows = seq // GRID_W
    row, col = jnp.meshgrid(jnp.arange(rows), jnp.arange(GRID_W), indexing='ij')
    row = row.reshape(-1).astype(F32)
    col = col.reshape(-1).astype(F32)
    inv_freq = 1.0 / (ROPE_BASE ** (jnp.arange(0, half, 2, dtype=F32) / half))

    def table(pos):
        ang = pos[:, None] * inv_freq[None, :]
        ang = jnp.concatenate([ang, ang], axis=-1)
        return jnp.cos(ang), jnp.sin(ang)
    cr, sr = table(row)
    cc, scol = table(col)
    cos = jnp.concatenate([cr, cc], -1)
    sin = jnp.concatenate([sr, scol], -1)
    quarter = (jnp.arange(dim) // (half // 2)) % 2
    sa = jnp.where(quarter[None, :] == 0, -sin, 0.0)
    sb = jnp.where(quarter[None, :] == 1, sin, 0.0)
    ctx1 = jnp.ones((CTX_LEN, dim), F32)
    ctx0 = jnp.zeros((CTX_LEN, dim), F32)
    return tuple(jnp.tile(jnp.concatenate([c, x], axis=0), (1, LANES // dim))
                 for c, x in ((ctx1, cos), (ctx0, sa), (ctx0, sb)))


def token_mixer(i, s2d, mod, rope, p, *, nb):
    d = s2d.shape[1]
    kind = i % N_MIXERS
    j = i // N_MIXERS
    nw = p['norm_mix_w'][i]
    if kind == 0:
        lambda_init = 0.8 - 0.6 * math.exp(-0.3 * i)
        qkv = proj(s2d, nw, mod, p['da_w_qkv'][j].astype(BF16), nb=nb, tn=512, out_dtype=BF16,
                   epilogue="rope", rope=rope, rope_tiles=2 * d // 512)
        a = diff_attention(qkv, p['da_lambda'][j], p['da_subln_w'][j], nb=nb,
                           lambda_init=lambda_init)
        return out_proj(a, p['da_w_o'][j].astype(BF16), mod, s2d, nb=nb, gate_chunk=2)
    if kind == 1:
        w_in = p['dn_w_in'][j]
        qkv_pre = proj(s2d, nw, mod, w_in[:, :DN_QKV].astype(BF16), nb=nb, tn=512, out_dtype=F32)
        zs = proj(s2d, nw, mod, w_in[:, DN_QKV:DN_QKV + DN_VD].astype(BF16), nb=nb, tn=512,
                  out_dtype=BF16, epilogue="silu")
        ba = proj(s2d, nw, mod, w_in[:, DN_QKV + DN_VD:], nb=nb, tn=LANES, out_dtype=F32)
        qkv = dn_conv(qkv_pre, p['dn_conv_w'][j], nb=nb)
        beta, gc, gt = dn_gates(ba, p['dn_a_log'][j], p['dn_dt_bias'][j], nb=nb)
        gcol, grow = _dn_gate_layouts(beta, gc, gt, nb=nb)
        o2 = dn_delta(qkv, gcol, grow, nb=nb)
        a = dn_gatenorm(o2, zs, p['dn_norm_w'][j])
        return out_proj(a, p['dn_w_o'][j].astype(BF16), mod, s2d, nb=nb, gate_chunk=2)
    qkv = proj(s2d, nw, mod, p['wa_w_qkv'][j].astype(BF16), nb=nb, tn=256, out_dtype=BF16,
               epilogue="rope", rope=rope, rope_tiles=(WA_HQ + WA_HKV) * WA_DIM // 256)
    a = window_attention(qkv, p['wa_sinks'][j], nb=nb)
    return out_proj(a, p['wa_w_o'][j].astype(BF16), mod, s2d, nb=nb, gate_chunk=2)


def moe_sublayer(i, s2d, mod, p, *, nb):
    return hier_moe(s2d, p['norm_ffn_w'][i], mod, p['moe_wg'][i], p['moe_bg'][i], p['moe_we'][i],
                    p['moe_be'][i], p['moe_w13_bf16'], p['moe_w2_bf16'], i, nb=nb)


def kernel(x, c, ctx, c_ctx, ada_w, ada_b, norm_mix_w, norm_ffn_w, da_w_qkv, da_lambda, da_subln_w, da_w_o, dn_w_in, dn_conv_w, dn_a_log, dn_dt_bias, dn_norm_w, dn_w_o, wa_w_qkv, wa_sinks, wa_w_o, moe_wg, moe_bg, moe_we, moe_be, moe_w13, moe_w2, final_norm_w):
    p = dict(norm_mix_w=norm_mix_w, norm_ffn_w=norm_ffn_w, da_w_qkv=da_w_qkv,
             da_lambda=da_lambda, da_subln_w=da_subln_w, da_w_o=da_w_o, dn_w_in=dn_w_in,
             dn_conv_w=dn_conv_w, dn_a_log=dn_a_log, dn_dt_bias=dn_dt_bias, dn_norm_w=dn_norm_w,
             dn_w_o=dn_w_o, wa_w_qkv=wa_w_qkv, wa_sinks=wa_sinks, wa_w_o=wa_w_o, moe_wg=moe_wg,
             moe_bg=moe_bg, moe_we=moe_we, moe_be=moe_be,
             moe_w13_bf16=moe_w13.astype(BF16), moe_w2_bf16=moe_w2.astype(BF16))
    nb, seq, d = x.shape
    t = nb * (CTX_LEN + seq)
    s2d = jnp.concatenate([ctx, x], axis=1).reshape(t, d)
    cc = jnp.concatenate([c, c_ctx[None, :], jnp.zeros((16 - nb - 1, d), F32)], axis=0)
    mods = ada_tables(cc, ada_w, ada_b).reshape(DEPTH, 16, 1, 6 * d)
    rope = _rope_tables(seq)
    for i in range(DEPTH):
        s2d = token_mixer(i, s2d, mods[i], rope, p, nb=nb)
        s2d = moe_sublayer(i, s2d, mods[i], p, nb=nb)
    return final_norm(s2d, final_norm_w, nb=nb)
```

```python
import functools
import math

import jax
import jax.numpy as jnp
from jax import lax
from jax.experimental import pallas as pl
from jax.experimental.pallas import tpu as pltpu

F32 = jnp.float32
BF16 = jnp.bfloat16
HIGHEST = lax.Precision.HIGHEST

D_MODEL = 2048
DEPTH = 4
GRID_W = 64
CTX_LEN = 256
N_MIXERS = 3
EPS = 1e-6
ROPE_BASE = 10000.0
NEG_INF = -1e30
DA_HEADS = 16
DA_DIM = 64
DN_HK = 16
DN_HV = 32
DN_DK = 128
DN_CONV = 5
DN_CHUNK = 64
DN_QK = DN_HK * DN_DK
DN_VD = DN_HV * DN_DK
DN_QKV = 2 * DN_QK + DN_VD
WA_HQ = 32
WA_HKV = 4
WA_GROUP = 8
WA_DIM = 64
WINDOW = 128
N_GROUPS = 4
EXPERTS_PER_GROUP = 8
N_EXPERTS = 32
TOP_K = 2
D_EXPERT = 768

LANES = 128
TM = 256
BIG_TILES_PER_BATCH = 3
MOE_ROWS = 256
DN_G = 16
DA_TQ = 512
VMEM_LIMIT = 56 * 1024 * 1024


def _cparams(n_axes):
    return pltpu.CompilerParams(dimension_semantics=("arbitrary",) * n_axes,
                                vmem_limit_bytes=VMEM_LIMIT)


def _mm(a, b):
    return jnp.dot(a.astype(BF16), b.astype(BF16), preferred_element_type=F32)


def _mm_nt(a, b):
    return lax.dot_general(a.astype(BF16), b.astype(BF16), (((1,), (1,)), ((), ())),
                           preferred_element_type=F32)


def _split_bf16(a):
    hi = a.astype(BF16)
    return hi, (a - hi.astype(F32)).astype(BF16)


def _mm_presplit(a_parts, b_parts):
    b_hi, b_lo = b_parts
    m = a_parts[0].shape[0]
    o_hi = jnp.dot(jnp.concatenate(a_parts, axis=0), b_hi, preferred_element_type=F32)
    o_lo = jnp.dot(a_parts[0], b_lo, preferred_element_type=F32)
    return (o_hi[:m] + o_hi[m:]) + o_lo


def _sigmoid(x):
    return 1.0 / (1.0 + jnp.exp(-x))


def _mod_spec(nb, tpb, chunk, width):
    def imap(r, *_):
        return (jnp.where(r % tpb == 0, nb, r // tpb), 0, chunk)
    return pl.BlockSpec((1, 1, width), imap)


def _big_mod_specs(nb, tpb, chunk, width, col=None):
    def blk(j):
        return chunk if col is None else chunk * col + j
    return [pl.BlockSpec((1, 1, width), lambda r, j: (nb, 0, blk(j))),
            pl.BlockSpec((1, 1, width), lambda r, j: (r // tpb, 0, blk(j)))]


def _ctx_rows(rows, tiles_per_batch):
    first = (pl.program_id(0) % tiles_per_batch) * rows
    return first + lax.broadcasted_iota(jnp.int32, (rows, 1), 0) < CTX_LEN


def _ada_kernel(c_ref, w_ref, b_ref, o_ref):
    c = c_ref[...]
    o_ref[0] = jnp.dot(c * _sigmoid(c), w_ref[0], preferred_element_type=F32,
                       precision=HIGHEST) + b_ref[0]


def ada_tables(cc, ada_w, ada_b):
    depth, d, n = ada_w.shape
    tn = 768
    return pl.pallas_call(
        _ada_kernel,
        grid=(depth, n // tn),
        in_specs=[pl.BlockSpec((16, d), lambda l, j: (0, 0)),
                  pl.BlockSpec((1, d, tn), lambda l, j: (l, 0, j)),
                  pl.BlockSpec((1, 1, tn), lambda l, j: (l, 0, j))],
        out_specs=pl.BlockSpec((1, 16, tn), lambda l, j: (l, 0, j)),
        out_shape=jax.ShapeDtypeStruct((depth, 16, n), F32),
        compiler_params=_cparams(2),
        name="ada_tables",
    )(cc, ada_w, ada_b.reshape(depth, 1, n))


def _norm_mod(x, nw, sc, sh):
    y = x * lax.rsqrt(jnp.mean(x * x, axis=-1, keepdims=True) + EPS) * nw
    return y * (1.0 + sc) + sh


def _proj_kernel(*refs, epilogue, rope_tiles, tpb):
    if epilogue == "rope":
        (x_ref, nw_ref, scc_ref, scx_ref, shc_ref, shx_ref, w_ref, cos_ref, sa_ref, sb_ref,
         o_ref, h_scr) = refs
    else:
        x_ref, nw_ref, scc_ref, scx_ref, shc_ref, shx_ref, w_ref, o_ref, h_scr = refs
    j = pl.program_id(1)

    @pl.when(j == 0)
    def _():
        rows = x_ref.shape[0]
        first = pl.program_id(0) % tpb == 0
        sc_top = jnp.where(first, scc_ref[0], scx_ref[0])
        sh_top = jnp.where(first, shc_ref[0], shx_ref[0])
        h_scr[0:CTX_LEN, :] = _norm_mod(x_ref[0:CTX_LEN, :], nw_ref[...], sc_top,
                                        sh_top).astype(h_scr.dtype)
        if rows > CTX_LEN:
            h_scr[CTX_LEN:rows, :] = _norm_mod(x_ref[CTX_LEN:rows, :], nw_ref[...], scx_ref[0],
                                               shx_ref[0]).astype(h_scr.dtype)

    precision = HIGHEST if w_ref.dtype == F32 else None
    acc = jnp.dot(h_scr[...], w_ref[...], preferred_element_type=F32, precision=precision)
    if epilogue == "rope":
        @pl.when(j < rope_tiles)
        def _():
            cos, sa, sb = cos_ref[...], sa_ref[...], sb_ref[...]
            for g in range(acc.shape[1] // LANES):
                a = acc[:, g * LANES:(g + 1) * LANES]
                rot = a * cos + pltpu.roll(a, LANES - 16, 1) * sa + pltpu.roll(a, 16, 1) * sb
                o_ref[:, g * LANES:(g + 1) * LANES] = rot.astype(o_ref.dtype)

        @pl.when(j >= rope_tiles)
        def _():
            o_ref[...] = acc.astype(o_ref.dtype)
    elif epilogue == "silu":
        o_ref[...] = (acc * _sigmoid(acc)).astype(o_ref.dtype)
    else:
        o_ref[...] = acc.astype(o_ref.dtype)


def proj(s2d, nw, mod, w, *, nb, tn, out_dtype, epilogue=None, rope=None, rope_tiles=0):
    t, d = s2d.shape
    n = w.shape[1]
    tpb = BIG_TILES_PER_BATCH
    tm = t // nb // tpb
    in_specs = ([pl.BlockSpec((tm, d), lambda r, j: (r, 0)),
                 pl.BlockSpec((1, d), lambda r, j: (0, 0))]
                + _big_mod_specs(nb, tpb, 1, d) + _big_mod_specs(nb, tpb, 0, d)
                + [pl.BlockSpec((d, tn), lambda r, j: (0, j))])
    args = [s2d, nw.reshape(1, d), mod, mod, mod, mod, w]
    if epilogue == "rope":
        tab = pl.BlockSpec((tm, LANES), lambda r, j: (r % tpb, 0))
        in_specs += [tab, tab, tab]
        args += list(rope)
    return pl.pallas_call(
        functools.partial(_proj_kernel, epilogue=epilogue, rope_tiles=rope_tiles, tpb=tpb),
        grid=(t // tm, n // tn),
        in_specs=in_specs,
        out_specs=pl.BlockSpec((tm, tn), lambda r, j: (r, j)),
        out_shape=jax.ShapeDtypeStruct((t, n), out_dtype),
        scratch_shapes=[pltpu.VMEM((tm, d), w.dtype)],
        compiler_params=_cparams(2),
        name="proj",
    )(*args)


def _out_kernel(a_ref, w_ref, gc_ref, gx_ref, s_ref, o_ref, *, tpb):
    acc = jnp.dot(a_ref[...], w_ref[...], preferred_element_type=F32)
    gate = jnp.where(_ctx_rows(a_ref.shape[0], tpb), gc_ref[0], gx_ref[0])
    o_ref[...] = s_ref[...] + gate * acc


def out_proj(a, w, mod, s2d, *, nb, gate_chunk, tn=512):
    t, k = a.shape
    d = w.shape[1]
    tpb = BIG_TILES_PER_BATCH
    tm = t // nb // tpb
    return pl.pallas_call(
        functools.partial(_out_kernel, tpb=tpb),
        grid=(t // tm, d // tn),
        in_specs=[pl.BlockSpec((tm, k), lambda r, j: (r, 0)),
                  pl.BlockSpec((k, tn), lambda r, j: (0, j))]
                 + _big_mod_specs(nb, tpb, gate_chunk, tn, col=d // tn)
                 + [pl.BlockSpec((tm, tn), lambda r, j: (r, j))],
        out_specs=pl.BlockSpec((tm, tn), lambda r, j: (r, j)),
        out_shape=jax.ShapeDtypeStruct((t, d), F32),
        compiler_params=_cparams(2),
        name="out_proj",
    )(a, w, mod, mod, s2d)


def _da_kernel(lam_ref, q_ref, k_ref, v_ref, sw_ref, o_ref, klo, khi, *, lambda_init, n_ctx, tq):
    lam = lam_ref[...]
    l01 = jnp.sum(lam[0:1] * lam[1:2], axis=-1, keepdims=True)
    l23 = jnp.sum(lam[2:3] * lam[3:4], axis=-1, keepdims=True)
    lam_full = jnp.exp(l01) - jnp.exp(l23) + lambda_init
    lane = lax.broadcasted_iota(jnp.int32, (1, LANES), 1)
    k = k_ref[...]
    zero = jnp.zeros_like(k)
    klo[...] = jnp.where(lane < DA_DIM, k, zero)
    khi[...] = jnp.where(lane >= DA_DIM, k, zero)

    def attend(row0, rows, nk):
        q = q_ref[pl.ds(row0, rows), :] * (DA_DIM ** -0.5)
        v = v_ref[0:nk, :]
        s1 = _mm_nt(q, klo[0:nk, :])
        s2 = _mm_nt(q, khi[0:nk, :])
        e1 = jnp.exp(s1 - jnp.max(s1, axis=-1, keepdims=True))
        e2 = jnp.exp(s2 - jnp.max(s2, axis=-1, keepdims=True))
        o1 = _mm(e1, v) / jnp.sum(e1, axis=-1, keepdims=True)
        o2 = _mm(e2, v) / jnp.sum(e2, axis=-1, keepdims=True)
        o = o1 - lam_full * o2
        o = o * lax.rsqrt(jnp.mean(o * o, axis=-1, keepdims=True) + EPS) * sw_ref[...]
        o_ref[pl.ds(row0, rows), :] = (o * (1.0 - lambda_init)).astype(o_ref.dtype)

    attend(0, n_ctx, n_ctx)

    def latent_tile(i, carry):
        attend(pl.multiple_of(n_ctx + i * tq, LANES), tq, k_ref.shape[0])
        return carry
    lax.fori_loop(0, (q_ref.shape[0] - n_ctx) // tq, latent_tile, 0)


def diff_attention(qkv, lam, subln_w, *, nb, lambda_init):
    t = qkv.shape[0]
    lt = t // nb
    col = pl.BlockSpec((lt, LANES), lambda b, h: (b, h))
    return pl.pallas_call(
        functools.partial(_da_kernel, lambda_init=lambda_init, n_ctx=CTX_LEN, tq=DA_TQ),
        grid=(nb, DA_HEADS),
        in_specs=[pl.BlockSpec((4, DA_DIM), lambda b, h: (0, 0)),
                  col,
                  pl.BlockSpec((lt, LANES), lambda b, h: (b, DA_HEADS + h)),
                  pl.BlockSpec((lt, LANES), lambda b, h: (b, 2 * DA_HEADS + h)),
                  pl.BlockSpec((1, LANES), lambda b, h: (0, 0))],
        out_specs=col,
        out_shape=jax.ShapeDtypeStruct((t, D_MODEL), BF16),
        scratch_shapes=[pltpu.VMEM((lt, LANES), BF16)] * 2,
        compiler_params=_cparams(2),
        name="diff_attention",
    )(lam, qkv, qkv, qkv, subln_w.reshape(1, LANES))


def _wa_kernel(sink_ref, q_ref, k_ref, v_ref, o_ref, klo, khi, vlo, vhi, *, n_ctx, seq):
    hkv = pl.program_id(1)
    i = pl.program_id(2)
    tq = q_ref.shape[0]
    band = tq + 2 * WINDOW
    npair = WA_GROUP // 2
    lane = lax.broadcasted_iota(jnp.int32, (1, LANES), 1)
    n_ctx_tiles = n_ctx // tq

    @pl.when(i == 0)
    def _():
        odd = (hkv % 2) == 1
        for src, lo, hi in ((k_ref, klo, khi), (v_ref, vlo, vhi)):
            x = src[...].astype(F32)
            x = jnp.where(odd, pltpu.roll(x, WA_DIM, 1), x)
            x = jnp.where(lane < WA_DIM, x, 0.0)
            lo[...] = x.astype(BF16)
            hi[...] = pltpu.roll(x, WA_DIM, 1).astype(BF16)

    def attend(with_band):
        halves = 2
        hp = npair // halves
        rows = hp * tq
        qs = [jnp.concatenate([q_ref[:, p * LANES:(p + 1) * LANES]
                               for p in range(c * hp, (c + 1) * hp)], axis=0) * (WA_DIM ** -0.5)
              for c in range(halves)]
        chains = [(c, par) for c in range(halves) for par in range(2)]
        kv = ((klo, vlo), (khi, vhi))
        sink = [jnp.concatenate([jnp.full((tq, 1), sink_ref[hkv, 2 * p + par], F32)
                                 for p in range(c * hp, (c + 1) * hp)], axis=0)
                for c, par in chains]
        s_c = [_mm_nt(qs[c], kv[par][0][0:n_ctx, :]) for c, par in chains]
        m = [jnp.maximum(jnp.max(s_c[n], axis=-1, keepdims=True), sink[n])
             for n in range(len(chains))]
        if with_band:
            lat0 = (i - n_ctx_tiles) * tq
            start = jnp.clip(lat0 - WINDOW, 0, seq - band)
            row0 = pl.multiple_of(n_ctx + start, LANES)
            qpos = lat0 + (lax.broadcasted_iota(jnp.int32, (rows, band), 0) & (tq - 1))
            kpos = start + lax.broadcasted_iota(jnp.int32, (rows, band), 1)
            valid = jnp.abs(qpos - kpos) <= WINDOW
            s_b = [jnp.where(valid, _mm_nt(qs[c], kv[par][0][pl.ds(row0, band), :]), NEG_INF)
                   for c, par in chains]
            m = [jnp.maximum(m[n], jnp.max(s_b[n], axis=-1, keepdims=True))
                 for n in range(len(chains))]
            e_b = [jnp.exp(s_b[n] - m[n]) for n in range(len(chains))]
        e_c = [jnp.exp(s_c[n] - m[n]) for n in range(len(chains))]
        den = [jnp.sum(e_c[n], axis=-1, keepdims=True) + jnp.exp(sink[n] - m[n])
               for n in range(len(chains))]
        acc = [_mm(e_c[n], kv[par][1][0:n_ctx, :]) for n, (c, par) in enumerate(chains)]
        if with_band:
            den = [den[n] + jnp.sum(e_b[n], axis=-1, keepdims=True) for n in range(len(chains))]
            acc = [acc[n] + _mm(e_b[n], kv[par][1][pl.ds(row0, band), :])
                   for n, (c, par) in enumerate(chains)]
        out = [acc[2 * c] / den[2 * c] + acc[2 * c + 1] / den[2 * c + 1] for c in range(halves)]
        for p in range(npair):
            c, r = divmod(p, hp)
            o_ref[:, p * LANES:(p + 1) * LANES] = out[c][r * tq:(r + 1) * tq].astype(o_ref.dtype)

    @pl.when(i < n_ctx_tiles)
    def _():
        attend(False)

    @pl.when(i >= n_ctx_tiles)
    def _():
        attend(True)


def window_attention(qkv, sinks, *, nb):
    t = qkv.shape[0]
    lt = t // nb
    tq = 128
    nq = lt // tq
    qw = WA_GROUP * WA_DIM
    kblk = WA_HQ * WA_DIM // LANES
    vblk = kblk + WA_HKV * WA_DIM // LANES
    return pl.pallas_call(
        functools.partial(_wa_kernel, n_ctx=CTX_LEN, seq=lt - CTX_LEN),
        grid=(nb, WA_HKV, nq),
        in_specs=[pl.BlockSpec(memory_space=pltpu.SMEM),
                  pl.BlockSpec((tq, qw), lambda b, h, i: (b * nq + i, h)),
                  pl.BlockSpec((lt, LANES), lambda b, h, i: (b, kblk + h // 2)),
                  pl.BlockSpec((lt, LANES), lambda b, h, i: (b, vblk + h // 2))],
        out_specs=pl.BlockSpec((tq, qw), lambda b, h, i: (b * nq + i, h)),
        out_shape=jax.ShapeDtypeStruct((t, D_MODEL), BF16),
        scratch_shapes=[pltpu.VMEM((lt, LANES), BF16)] * 4,
        compiler_params=_cparams(3),
        name="window_attention",
    )(sinks.reshape(WA_HKV, WA_GROUP), qkv, qkv, qkv)


def _dn_conv_kernel(x_ref, w_ref, o_ref, *, n_ctx):
    cb = pl.program_id(1)
    x = x_ref[...]
    n = x.shape[0]
    w = w_ref[...]
    t = lax.broadcasted_iota(jnp.int32, (n, 1), 0)
    lo = jnp.where(t < n_ctx, 0, n_ctx)
    hi = jnp.where(t < n_ctx, n_ctx, n)
    acc = jnp.zeros_like(x)
    for j in range(DN_CONV):
        sft = j - (DN_CONV - 1) // 2
        xs = x if sft == 0 else pltpu.roll(x, (-sft) % n, 0)
        u = t + sft
        ok = (u >= lo) & (u < hi)
        acc = acc + jnp.where(ok, xs, 0.0) * w[j:j + 1, :]
    y = acc * _sigmoid(acc)
    nrm = lax.rsqrt(jnp.sum(y * y, axis=-1, keepdims=True) + EPS)
    scl = jnp.where(cb < DN_HK, nrm * (DN_DK ** -0.5), jnp.where(cb < 2 * DN_HK, nrm, 1.0))
    o_ref[...] = y * scl


def dn_conv(qkv_pre, conv_w, *, nb):
    t, n = qkv_pre.shape
    lt = t // nb
    return pl.pallas_call(
        functools.partial(_dn_conv_kernel, n_ctx=CTX_LEN),
        grid=(nb, n // LANES),
        in_specs=[pl.BlockSpec((lt, LANES), lambda b, c: (b, c)),
                  pl.BlockSpec((DN_CONV, LANES), lambda b, c: (0, c))],
        out_specs=pl.BlockSpec((lt, LANES), lambda b, c: (b, c)),
        out_shape=jax.ShapeDtypeStruct((t, n), F32),
        compiler_params=_cparams(2),
        name="dn_conv",
    )(qkv_pre, conv_w)


def _dn_gate_kernel(ba_ref, alog_ref, dtb_ref, beta_ref, gc_ref, gt_ref):
    n = ba_ref.shape[0]
    c = DN_CHUNK
    ba = ba_ref[...]
    beta_ref[...] = _sigmoid(ba)
    z = ba + dtb_ref[...]
    softplus = jnp.maximum(z, 0.0) + jnp.log(1.0 + jnp.exp(-jnp.abs(z)))
    g = -jnp.exp(alog_ref[...]) * softplus
    row = lax.broadcasted_iota(jnp.int32, (c, c), 0)
    col = lax.broadcasted_iota(jnp.int32, (c, c), 1)
    prefix = (col <= row).astype(F32)
    suffix = (col >= row).astype(F32)
    ones = jnp.ones((c, c), F32)
    fwd_lane = lax.broadcasted_iota(jnp.int32, (1, LANES), 1) < 2 * DN_HV
    for ci in range(n // c):
        gch = g[ci * c:(ci + 1) * c, :]
        csum = jnp.where(fwd_lane,
                         jnp.dot(prefix, gch, preferred_element_type=F32, precision=HIGHEST),
                         jnp.dot(suffix, gch, preferred_element_type=F32, precision=HIGHEST))
        gc_ref[ci * c:(ci + 1) * c, :] = csum
        gt_ref[ci * c:(ci + 1) * c, :] = jnp.dot(ones, gch, preferred_element_type=F32,
                                                 precision=HIGHEST)


def dn_gates(ba, a_log, dt_bias, *, nb):
    t = ba.shape[0]
    lt = t // nb
    zeros = jnp.zeros((DN_HV,), F32)
    alog = jnp.concatenate([zeros, a_log[0], zeros, a_log[1]]).reshape(1, LANES)
    dtb = jnp.concatenate([zeros, dt_bias[0], zeros, dt_bias[1]]).reshape(1, LANES)
    blk = pl.BlockSpec((lt, LANES), lambda b: (b, 0))
    vec = pl.BlockSpec((1, LANES), lambda b: (0, 0))
    shp = jax.ShapeDtypeStruct((t, LANES), F32)
    return pl.pallas_call(
        _dn_gate_kernel,
        grid=(nb,),
        in_specs=[blk, vec, vec],
        out_specs=[blk, blk, blk],
        out_shape=[shp, shp, shp],
        compiler_params=_cparams(1),
        name="dn_gates",
    )(ba, alog, dtb)


def _dn_chunk_index(d, c, n_ctx_chunks, n_chunks):
    bwd = jnp.where(c < n_ctx_chunks, n_ctx_chunks - 1 - c, n_chunks - 1 + n_ctx_chunks - c)
    return jnp.where(d == 0, c, bwd)


def _dn_delta_kernel(q_ref, k_ref, v_ref, gcol_ref, grow_ref, o_ref, s_scr):
    d = pl.program_id(2)
    c = pl.program_id(3)
    cs = DN_CHUNK

    @pl.when(c == 0)
    def _():
        s_scr[...] = jnp.zeros_like(s_scr)

    row = lax.broadcasted_iota(jnp.int32, (cs, cs), 0)
    col = lax.broadcasted_iota(jnp.int32, (cs, cs), 1)
    ahead = (row - col) * jnp.where(d == 0, 1, -1)
    incl = ahead >= 0
    strict = ahead > 0
    eye = (row == col).astype(F32)

    gates = gcol_ref[0, 0, 0]
    grows = grow_ref[0, 0, 0, 0]
    heads = range(DN_G)
    qs = [q_ref[:, hk * DN_DK:(hk + 1) * DN_DK] for hk in range(DN_G // 2)]
    ks = [k_ref[:, hk * DN_DK:(hk + 1) * DN_DK] for hk in range(DN_G // 2)]
    vs = [v_ref[:, h * DN_DK:(h + 1) * DN_DK] for h in heads]
    kks = [_mm_nt(k, k) for k in ks]
    qks = [_mm_nt(q, k) for q, k in zip(qs, ks)]
    beta = [gates[:, h:h + 1] for h in heads]
    gc = [gates[:, DN_G + h:DN_G + h + 1] for h in heads]
    gt = [gates[:, 2 * DN_G + h:2 * DN_G + h + 1] for h in heads]
    decay = [jnp.exp(jnp.where(incl, gc[h] - grows[h:h + 1, :], NEG_INF)) for h in heads]
    pw = [-jnp.where(strict, kks[h // 2] * (beta[h] * decay[h]), 0.0) for h in heads]
    tinv = [eye + pw[h] for h in heads]
    sp = [_split_bf16(pw[h]) for h in heads]
    for _ in range(5):
        sp = [_split_bf16(_mm_presplit(sp[h], sp[h])) for h in heads]
        tinv = [tinv[h] + _mm_presplit(_split_bf16(tinv[h]), sp[h]) for h in heads]
    eg = [jnp.exp(gc[h]) for h in heads]
    uw = [_mm(tinv[h], jnp.concatenate([vs[h] * beta[h], ks[h // 2] * (beta[h] * eg[h])], axis=1))
          for h in heads]
    s_old = [s_scr[h] for h in heads]
    v_new = [uw[h][:, :DN_DK] - _mm(uw[h][:, DN_DK:], s_old[h]) for h in heads]
    outs = [_mm(qs[h // 2] * eg[h], s_old[h])
            + _mm(jnp.where(incl, qks[h // 2] * decay[h], 0.0), v_new[h]) for h in heads]
    o_ref[0] = jnp.concatenate(outs, axis=1)
    kd = [ks[h // 2] * jnp.exp(gt[h] - gc[h]) for h in heads]
    s_scr[...] = jnp.stack([s_old[h] * jnp.exp(gt[h][0:1, :]) + _mm(kd[h].T, v_new[h])
                            for h in heads], axis=0)


def dn_delta(qkv, gcol, grow, *, nb):
    t = qkv.shape[0]
    lt = t // nb
    nch = lt // DN_CHUNK
    nctx = CTX_LEN // DN_CHUNK
    ng = DN_HV // DN_G
    qw = DN_G // 2 * DN_DK
    vw = DN_G * DN_DK
    cidx = functools.partial(_dn_chunk_index, n_ctx_chunks=nctx, n_chunks=nch)
    return pl.pallas_call(
        _dn_delta_kernel,
        grid=(nb, ng, 2, nch),
        in_specs=[pl.BlockSpec((DN_CHUNK, qw), lambda b, g, d, c: (b * nch + cidx(d, c), g)),
                  pl.BlockSpec((DN_CHUNK, qw), lambda b, g, d, c: (b * nch + cidx(d, c),
                                                                   DN_QK // qw + g)),
                  pl.BlockSpec((DN_CHUNK, vw), lambda b, g, d, c: (b * nch + cidx(d, c),
                                                                   2 * DN_QK // vw + g)),
                  pl.BlockSpec((1, 1, 1, DN_CHUNK, 3 * DN_G),
                               lambda b, g, d, c: (b, d, g, cidx(d, c), 0)),
                  pl.BlockSpec((1, 1, 1, 1, DN_G, DN_CHUNK),
                               lambda b, g, d, c: (b, d, g, cidx(d, c), 0, 0))],
        out_specs=pl.BlockSpec((1, DN_CHUNK, vw), lambda b, g, d, c: (d, b * nch + cidx(d, c), g)),
        out_shape=jax.ShapeDtypeStruct((2, t, DN_VD), F32),
        scratch_shapes=[pltpu.VMEM((DN_G, DN_DK, DN_DK), F32)],
        compiler_params=_cparams(4),
        name="dn_delta",
    )(qkv, qkv, qkv, gcol, grow)


def _dn_gate_layouts(beta, gc, gt, *, nb):
    t = beta.shape[0]
    lt = t // nb
    nch = lt // DN_CHUNK
    ng = DN_HV // DN_G
    cols, rows = [], []
    for d in range(2):
        b0 = 2 * d * DN_HV
        parts = [beta[:, b0:b0 + DN_HV], gc[:, b0 + DN_HV:b0 + 2 * DN_HV],
                 gt[:, b0 + DN_HV:b0 + 2 * DN_HV]]
        parts = [p.reshape(nb, lt, ng, DN_G).transpose(0, 2, 1, 3) for p in parts]
        cols.append(jnp.concatenate(parts, axis=-1))
        rows.append(parts[1].reshape(nb, ng, nch, DN_CHUNK, DN_G).transpose(0, 1, 2, 4, 3))
    return jnp.stack(cols, axis=1), jnp.stack(rows, axis=1)


def _dn_gatenorm_kernel(o_ref, zs_ref, nw_ref, a_ref):
    for h in range(DN_HV):
        sl = slice(h * DN_DK, (h + 1) * DN_DK)
        o = o_ref[0, :, sl] + o_ref[1, :, sl]
        y = o * lax.rsqrt(jnp.mean(o * o, axis=-1, keepdims=True) + EPS) * nw_ref[...]
        a_ref[:, sl] = (y * zs_ref[:, sl].astype(F32)).astype(a_ref.dtype)


def dn_gatenorm(o2, zs, norm_w):
    _, t, k = o2.shape
    return pl.pallas_call(
        _dn_gatenorm_kernel,
        grid=(t // TM,),
        in_specs=[pl.BlockSpec((2, TM, k), lambda r: (0, r, 0)),
                  pl.BlockSpec((TM, k), lambda r: (r, 0)),
                  pl.BlockSpec((1, DN_DK), lambda r: (0, 0))],
        out_specs=pl.BlockSpec((TM, k), lambda r: (r, 0)),
        out_shape=jax.ShapeDtypeStruct((t, k), BF16),
        compiler_params=_cparams(1),
        name="dn_gatenorm",
    )(o2, zs, norm_w.reshape(1, DN_DK))


def _router_kernel(x_ref, nw_ref, sc_ref, sh_ref, wr_ref, br_ref, hp_ref, eid_ref, gate_ref,
                   cnt_ref):
    h = _norm_mod(x_ref[...], nw_ref[...], sc_ref[0], sh_ref[0])
    logits = jnp.dot(h, wr_ref[...], preferred_element_type=F32, precision=HIGHEST) + br_ref[...]
    bits = pltpu.bitcast(h.astype(BF16).astype(F32), jnp.uint32)
    half = bits.shape[1] // 2
    hp_ref[...] = bits[:, :half] | (bits[:, half:] >> 16)
    lane = lax.broadcasted_iota(jnp.int32, logits.shape, 1)
    is_grp = lane < N_GROUPS
    mg = jnp.max(jnp.where(is_grp, logits, -jnp.inf), axis=-1, keepdims=True)
    eg = jnp.where(is_grp, jnp.exp(logits - mg), 0.0)
    pg = eg / jnp.sum(eg, axis=-1, keepdims=True)
    p_grp = jnp.max(pg, axis=-1, keepdims=True)
    g_sel = jnp.min(jnp.where(is_grp & (pg == p_grp), lane, LANES), axis=-1, keepdims=True)
    fine = lane - N_GROUPS
    in_grp = (fine >= 0) & (fine < N_EXPERTS) & ((fine // EXPERTS_PER_GROUP) == g_sel)
    mf = jnp.max(jnp.where(in_grp, logits, -jnp.inf), axis=-1, keepdims=True)
    ef = jnp.where(in_grp, jnp.exp(logits - mf), 0.0)
    pf = ef / jnp.sum(ef, axis=-1, keepdims=True)
    p1 = jnp.max(jnp.where(in_grp, pf, -1.0), axis=-1, keepdims=True)
    i1 = jnp.min(jnp.where(in_grp & (pf == p1), lane, LANES), axis=-1, keepdims=True)
    rest = in_grp & (lane != i1)
    p2 = jnp.max(jnp.where(rest, pf, -1.0), axis=-1, keepdims=True)
    i2 = jnp.min(jnp.where(rest & (pf == p2), lane, LANES), axis=-1, keepdims=True)
    den = p1 + p2
    gate_ref[...] = jnp.where(lane == 0, p_grp * p1 / den,
                              jnp.where(lane == 1, p_grp * p2 / den, 0.0))
    e1 = i1 - N_GROUPS
    e2 = i2 - N_GROUPS
    oh1 = jnp.where(lane == e1, 1.0, 0.0)
    oh2 = jnp.where(lane == e2, 1.0, 0.0)
    both = oh1 + oh2
    rows = both.shape[0]
    earlier = (lax.broadcasted_iota(jnp.int32, (rows, rows), 1)
               < lax.broadcasted_iota(jnp.int32, (rows, rows), 0))
    seen = jnp.dot(jnp.where(earlier, 1.0, 0.0).astype(BF16), both.astype(BF16),
                   preferred_element_type=F32)
    r1 = jnp.sum(seen * oh1, axis=-1, keepdims=True).astype(jnp.int32)
    r2 = jnp.sum(seen * oh2, axis=-1, keepdims=True).astype(jnp.int32)
    eid_ref[...] = jnp.where(lane == 0, e1, jnp.where(lane == 1, e2, jnp.where(
        lane == 2, r1, jnp.where(lane == 3, r2, 0))))
    cnt_ref[0] = jnp.broadcast_to(jnp.sum(both, axis=0, keepdims=True), (8, LANES)).astype(jnp.int32)


def moe_router(s2d, nw, mod, wr, br, *, nb, sc_chunk, sh_chunk):
    t, d = s2d.shape
    tpb = t // nb // TM
    row = pl.BlockSpec((TM, d), lambda r: (r, 0))
    small = pl.BlockSpec((TM, LANES), lambda r: (r, 0))
    return pl.pallas_call(
        _router_kernel,
        grid=(t // TM,),
        in_specs=[row, pl.BlockSpec((1, d), lambda r: (0, 0)),
                  _mod_spec(nb, tpb, sc_chunk, d), _mod_spec(nb, tpb, sh_chunk, d),
                  pl.BlockSpec((d, LANES), lambda r: (0, 0)),
                  pl.BlockSpec((1, LANES), lambda r: (0, 0))],
        out_specs=[pl.BlockSpec((TM, d // 2), lambda r: (r, 0)), small, small,
                   pl.BlockSpec((1, 8, LANES), lambda r: (r, 0, 0))],
        out_shape=[jax.ShapeDtypeStruct((t, d // 2), jnp.uint32),
                   jax.ShapeDtypeStruct((t, LANES), jnp.int32),
                   jax.ShapeDtypeStruct((t, LANES), F32),
                   jax.ShapeDtypeStruct((t // TM, 8, LANES), jnp.int32)],
        compiler_params=_cparams(1),
        name="moe_router",
    )(s2d, nw.reshape(1, d), mod, mod, wr, br)


def _row_copy(src_hbm, row, dst, r, sem):
    return pltpu.make_async_copy(src_hbm.at[pl.ds(row, 1), :], dst.at[pl.ds(r, 1), :], sem)


def _moe_gather_kernel(tok_ref, nused_ref, h_hbm, o_ref, buf, sem):
    blk = pl.program_id(0)
    rows = buf.shape[1]
    n_used = nused_ref[0]

    def fetch(b):
        slot = b % 2

        def issue(r, carry):
            _row_copy(h_hbm, tok_ref[b * rows + r], buf.at[slot], r, sem.at[slot]).start()
            return carry
        lax.fori_loop(0, rows, issue, 0, unroll=8)

    @pl.when(blk == 0)
    def _():
        fetch(blk)

    @pl.when(blk + 1 < n_used)
    def _():
        fetch(blk + 1)

    @pl.when(blk < n_used)
    def _():
        slot = blk % 2

        def wait(r, carry):
            _row_copy(h_hbm, 0, buf.at[slot], r, sem.at[slot]).wait()
            return carry
        lax.fori_loop(0, rows, wait, 0, unroll=8)
        o_ref[...] = buf[slot]

    @pl.when(blk >= n_used)
    def _():
        o_ref[...] = jnp.zeros_like(o_ref)


def moe_gather(h, buf_tok, n_used):
    t, d = h.shape
    p = buf_tok.shape[0]
    return pl.pallas_call(
        _moe_gather_kernel,
        grid_spec=pltpu.PrefetchScalarGridSpec(
            num_scalar_prefetch=2,
            grid=(p // MOE_ROWS,),
            in_specs=[pl.BlockSpec(memory_space=pl.ANY)],
            out_specs=pl.BlockSpec((MOE_ROWS, d), lambda i, *_: (i, 0)),
            scratch_shapes=[pltpu.VMEM((2, MOE_ROWS, d), h.dtype), pltpu.SemaphoreType.DMA((2,))]),
        out_shape=jax.ShapeDtypeStruct((p, d), h.dtype),
        compiler_params=_cparams(1),
        name="moe_gather",
    )(buf_tok, n_used, h)


def _moe_expert_kernel(be_ref, nused_ref, x_ref, w13_ref, w2_ref, o_ref):
    blk = pl.program_id(0)

    @pl.when(blk < nused_ref[0])
    def _():
        packed = x_ref[...]
        x = jnp.concatenate([pltpu.bitcast(packed & jnp.uint32(0xFFFF0000), F32),
                             pltpu.bitcast(packed << 16, F32)], axis=1).astype(BF16)
        h13 = jnp.dot(x, w13_ref[0, 0], preferred_element_type=F32)
        gate = h13[:, :D_EXPERT]
        act = gate * _sigmoid(gate) * h13[:, D_EXPERT:]
        o_ref[...] = jnp.dot(act.astype(BF16), w2_ref[0, 0], preferred_element_type=F32)

    @pl.when(blk >= nused_ref[0])
    def _():
        o_ref[...] = jnp.zeros_like(o_ref)


def moe_experts(xs, block_e, n_used, w13, w2, layer):
    p = xs.shape[0]
    d = w13.shape[2]
    return pl.pallas_call(
        _moe_expert_kernel,
        grid_spec=pltpu.PrefetchScalarGridSpec(
            num_scalar_prefetch=2,
            grid=(p // MOE_ROWS,),
            in_specs=[pl.BlockSpec((MOE_ROWS, d // 2), lambda i, be, nu: (i, 0)),
                      pl.BlockSpec((1, 1, d, 2 * D_EXPERT), lambda i, be, nu: (layer, be[i], 0, 0)),
                      pl.BlockSpec((1, 1, D_EXPERT, d), lambda i, be, nu: (layer, be[i], 0, 0))],
            out_specs=pl.BlockSpec((MOE_ROWS, d), lambda i, be, nu: (i, 0))),
        out_shape=jax.ShapeDtypeStruct((p, d), F32),
        compiler_params=_cparams(1),
        name="moe_experts",
    )(block_e, n_used, xs, w13, w2)


def _moe_combine_kernel(d0_ref, d1_ref, y_hbm, s_ref, gate_ref, g_ref, o_ref, buf, sem):
    i = pl.program_id(0)
    rows = buf.shape[2]

    def fetch(tile):
        slot = tile % 2

        def issue(r, carry):
            _row_copy(y_hbm, d0_ref[tile * rows + r], buf.at[slot, 0], r, sem.at[slot, 0]).start()
            _row_copy(y_hbm, d1_ref[tile * rows + r], buf.at[slot, 1], r, sem.at[slot, 1]).start()
            return carry
        lax.fori_loop(0, rows, issue, 0, unroll=8)

    @pl.when(i == 0)
    def _():
        fetch(i)

    @pl.when(i + 1 < pl.num_programs(0))
    def _():
        fetch(i + 1)

    slot = i % 2

    def wait(r, carry):
        _row_copy(y_hbm, 0, buf.at[slot, 0], r, sem.at[slot, 0]).wait()
        _row_copy(y_hbm, 0, buf.at[slot, 1], r, sem.at[slot, 1]).wait()
        return carry
    lax.fori_loop(0, rows, wait, 0, unroll=8)
    gates = gate_ref[...]
    y = gates[:, 0:1] * buf[slot, 0] + gates[:, 1:2] * buf[slot, 1]
    o_ref[...] = s_ref[...] + g_ref[0] * y


def moe_combine(yb, dest0, dest1, s2d, gates, mod, *, nb, gate_chunk):
    t, d = s2d.shape
    tpb = t // nb // TM
    return pl.pallas_call(
        _moe_combine_kernel,
        grid_spec=pltpu.PrefetchScalarGridSpec(
            num_scalar_prefetch=2,
            grid=(t // TM,),
            in_specs=[pl.BlockSpec(memory_space=pl.ANY),
                      pl.BlockSpec((TM, d), lambda r, *_: (r, 0)),
                      pl.BlockSpec((TM, LANES), lambda r, *_: (r, 0)),
                      _mod_spec(nb, tpb, gate_chunk, d)],
            out_specs=pl.BlockSpec((TM, d), lambda r, *_: (r, 0)),
            scratch_shapes=[pltpu.VMEM((2, 2, TM, d), F32), pltpu.SemaphoreType.DMA((2, 2))]),
        out_shape=jax.ShapeDtypeStruct((t, d), F32),
        compiler_params=_cparams(1),
        name="moe_combine",
    )(dest0, dest1, yb, s2d, gates, mod)


def _moe_plan(eid, tile_counts):
    t = eid.shape[0]
    tk = t * TOP_K
    experts = eid[:, :TOP_K]
    in_tile_rank = eid[:, TOP_K:2 * TOP_K]
    per_tile = tile_counts[:, 0, :N_EXPERTS]
    counts = jnp.sum(per_tile, axis=0)
    padded = (counts + MOE_ROWS - 1) // MOE_ROWS * MOE_ROWS
    pend = jnp.cumsum(padded)
    base = (pend - padded)[None, :] + jnp.cumsum(per_tile, axis=0) - per_tile
    base_rows = jnp.repeat(base, TM, axis=0)
    chosen = experts[:, :, None] == jnp.arange(N_EXPERTS, dtype=jnp.int32)[None, None, :]
    dest = jnp.sum(jnp.where(chosen, base_rows[:, None, :], 0), axis=-1) + in_tile_rank
    n_blocks = tk // MOE_ROWS + N_EXPERTS
    buf_tok = jnp.zeros((n_blocks * MOE_ROWS,), jnp.int32).at[dest.reshape(tk)].set(
        jnp.arange(tk, dtype=jnp.int32) // TOP_K)
    block_row0 = jnp.arange(n_blocks, dtype=jnp.int32) * MOE_ROWS
    block_e = jnp.sum((pend[None, :] <= block_row0[:, None]).astype(jnp.int32), axis=1)
    block_e = jnp.minimum(block_e, N_EXPERTS - 1).astype(jnp.int32)
    n_used = (pend[-1:] // MOE_ROWS).astype(jnp.int32)
    return buf_tok, block_e, n_used, dest[:, 0], dest[:, 1]


def hier_moe(s2d, nw, mod, wg, bg, we, be, w13, w2, layer, *, nb):
    d = s2d.shape[1]
    pad = LANES - N_GROUPS - N_EXPERTS
    wr = jnp.concatenate([wg, we, jnp.zeros((d, pad), F32)], axis=1)
    br = jnp.concatenate([bg, be, jnp.zeros((pad,), F32)]).reshape(1, LANES)
    hp, eid, gates, tile_counts = moe_router(s2d, nw, mod, wr, br, nb=nb, sc_chunk=4, sh_chunk=3)
    buf_tok, block_e, n_used, dest0, dest1 = _moe_plan(eid, tile_counts)
    xs = moe_gather(hp, buf_tok, n_used)
    yb = moe_experts(xs, block_e, n_used, w13, w2, layer)
    return moe_combine(yb, dest0, dest1, s2d, gates, mod, nb=nb, gate_chunk=5)


def _final_norm_kernel(x_ref, w_ref, o_ref):
    x = x_ref[...]
    o_ref[0] = x * lax.rsqrt(jnp.mean(x * x, axis=-1, keepdims=True) + EPS) * w_ref[...]


def final_norm(s2d, w, *, nb):
    t, d = s2d.shape
    lt = t // nb
    tpb = lt // TM
    ctx_tiles = CTX_LEN // TM
    return pl.pallas_call(
        _final_norm_kernel,
        grid=(nb, tpb - ctx_tiles),
        in_specs=[pl.BlockSpec((TM, d), lambda b, i: (b * tpb + ctx_tiles + i, 0)),
                  pl.BlockSpec((1, d), lambda b, i: (0, 0))],
        out_specs=pl.BlockSpec((1, TM, d), lambda b, i: (b, i, 0)),
        out_shape=jax.ShapeDtypeStruct((nb, lt - CTX_LEN, d), F32),
        compiler_params=_cparams(2),
        name="final_norm",
    )(s2d, w.reshape(1, d))


def _rope_tables(seq):
    dim = DA_DIM
    half = dim // 2
    rows = seq // GRID_W
    row, col = jnp.meshgrid(jnp.arange(rows), jnp.arange(GRID_W), indexing='ij')
    row = row.reshape(-1).astype(F32)
    col = col.reshape(-1).astype(F32)
    inv_freq = 1.0 / (ROPE_BASE ** (jnp.arange(0, half, 2, dtype=F32) / half))

    def table(pos):
        ang = pos[:, None] * inv_freq[None, :]
        ang = jnp.concatenate([ang, ang], axis=-1)
        return jnp.cos(ang), jnp.sin(ang)
    cr, sr = table(row)
    cc, scol = table(col)
    cos = jnp.concatenate([cr, cc], -1)
    sin = jnp.concatenate([sr, scol], -1)
    quarter = (jnp.arange(dim) // (half // 2)) % 2
    sa = jnp.where(quarter[None, :] == 0, -sin, 0.0)
    sb = jnp.where(quarter[None, :] == 1, sin, 0.0)
    ctx1 = jnp.ones((CTX_LEN, dim), F32)
    ctx0 = jnp.zeros((CTX_LEN, dim), F32)
    return tuple(jnp.tile(jnp.concatenate([c, x], axis=0), (1, LANES // dim))
                 for c, x in ((ctx1, cos), (ctx0, sa), (ctx0, sb)))


def token_mixer(i, s2d, mod, rope, p, *, nb):
    d = s2d.shape[1]
    kind = i % N_MIXERS
    j = i // N_MIXERS
    nw = p['norm_mix_w'][i]
    if kind == 0:
        lambda_init = 0.8 - 0.6 * math.exp(-0.3 * i)
        qkv = proj(s2d, nw, mod, p['da_w_qkv'][j].astype(BF16), nb=nb, tn=512, out_dtype=BF16,
                   epilogue="rope", rope=rope, rope_tiles=2 * d // 512)
        a = diff_attention(qkv, p['da_lambda'][j], p['da_subln_w'][j], nb=nb,
                           lambda_init=lambda_init)
        return out_proj(a, p['da_w_o'][j].astype(BF16), mod, s2d, nb=nb, gate_chunk=2)
    if kind == 1:
        w_in = p['dn_w_in'][j]
        qkv_pre = proj(s2d, nw, mod, w_in[:, :DN_QKV].astype(BF16), nb=nb, tn=512, out_dtype=F32)
        zs = proj(s2d, nw, mod, w_in[:, DN_QKV:DN_QKV + DN_VD].astype(BF16), nb=nb, tn=512,
                  out_dtype=BF16, epilogue="silu")
        ba = proj(s2d, nw, mod, w_in[:, DN_QKV + DN_VD:], nb=nb, tn=LANES, out_dtype=F32)
        qkv = dn_conv(qkv_pre, p['dn_conv_w'][j], nb=nb)
        beta, gc, gt = dn_gates(ba, p['dn_a_log'][j], p['dn_dt_bias'][j], nb=nb)
        gcol, grow = _dn_gate_layouts(beta, gc, gt, nb=nb)
        o2 = dn_delta(qkv, gcol, grow, nb=nb)
        a = dn_gatenorm(o2, zs, p['dn_norm_w'][j])
        return out_proj(a, p['dn_w_o'][j].astype(BF16), mod, s2d, nb=nb, gate_chunk=2)
    qkv = proj(s2d, nw, mod, p['wa_w_qkv'][j].astype(BF16), nb=nb, tn=256, out_dtype=BF16,
               epilogue="rope", rope=rope, rope_tiles=(WA_HQ + WA_HKV) * WA_DIM // 256)
    a = window_attention(qkv, p['wa_sinks'][j], nb=nb)
    return out_proj(a, p['wa_w_o'][j].astype(BF16), mod, s2d, nb=nb, gate_chunk=2)


def moe_sublayer(i, s2d, mod, p, *, nb):
    return hier_moe(s2d, p['norm_ffn_w'][i], mod, p['moe_wg'][i], p['moe_bg'][i], p['moe_we'][i],
                    p['moe_be'][i], p['moe_w13_bf16'], p['moe_w2_bf16'], i, nb=nb)


def kernel(x, c, ctx, c_ctx, ada_w, ada_b, norm_mix_w, norm_ffn_w, da_w_qkv, da_lambda, da_subln_w, da_w_o, dn_w_in, dn_conv_w, dn_a_log, dn_dt_bias, dn_norm_w, dn_w_o, wa_w_qkv, wa_sinks, wa_w_o, moe_wg, moe_bg, moe_we, moe_be, moe_w13, moe_w2, final_norm_w):
    p = dict(norm_mix_w=norm_mix_w, norm_ffn_w=norm_ffn_w, da_w_qkv=da_w_qkv,
             da_lambda=da_lambda, da_subln_w=da_subln_w, da_w_o=da_w_o, dn_w_in=dn_w_in,
             dn_conv_w=dn_conv_w, dn_a_log=dn_a_log, dn_dt_bias=dn_dt_bias, dn_norm_w=dn_norm_w,
             dn_w_o=dn_w_o, wa_w_qkv=wa_w_qkv, wa_sinks=wa_sinks, wa_w_o=wa_w_o, moe_wg=moe_wg,
             moe_bg=moe_bg, moe_we=moe_we, moe_be=moe_be,
             moe_w13_bf16=moe_w13.astype(BF16), moe_w2_bf16=moe_w2.astype(BF16))
    nb, seq, d = x.shape
    t = nb * (CTX_LEN + seq)
    s2d = jnp.concatenate([ctx, x], axis=1).reshape(t, d)
    cc = jnp.concatenate([c, c_ctx[None, :], jnp.zeros((16 - nb - 1, d), F32)], axis=0)
    mods = ada_tables(cc, ada_w, ada_b).reshape(DEPTH, 16, 1, 6 * d)
    rope = _rope_tables(seq)
    for i in range(DEPTH):
        s2d = token_mixer(i, s2d, mods[i], rope, p, nb=nb)
        s2d = moe_sublayer(i, s2d, mods[i], p, nb=nb)
    return final_norm(s2d, final_norm_w, nb=nb)
```
